```python
import math, functools
import jax, jax.numpy as jnp
from jax import lax
import numpy as np

D_MODEL = 1024
BATCH = 8
SEQ = 2048
DEPTH = 4
DEC_BATCH = 128
DEC_SEQ = 1
PAST_LEN = 2048
PAGE_SIZE = 128

N_MIXERS = 2
N_POOL_LAYERS = (DEPTH + 1) // 2
N_SB_LAYERS = DEPTH // 2
POOL_WINDOWS = (2, 4, 8, 16)
N_POOL_GROUPS = len(POOL_WINDOWS)
POOL_GROUP = D_MODEL // N_POOL_GROUPS
POOL_BUF = max(POOL_WINDOWS) - 1
N_HEADS = 16
HEAD_DIM = D_MODEL // N_HEADS
Q_BLOCK = 128
SB_BIAS_INIT = -6.0
D_FF = 2816
CONV_W = 3
PLE_DIM = 256
EPS = 1e-6

kernel_name = "pool_stickbreak_hybrid_step"


def rmsnorm(x, g):
    xf = x.astype(jnp.float32)
    y = xf * lax.rsqrt(jnp.mean(xf * xf, axis=-1, keepdims=True) + EPS)
    return (y * g.astype(jnp.float32)).astype(x.dtype)


def pool_mix(h, buf, pos0, w_grp, scale):
    B, T, _ = h.shape
    cat = jnp.concatenate([buf.astype(h.dtype), h], axis=1)
    csum = jnp.cumsum(cat.astype(jnp.float32), axis=1)
    csum = jnp.concatenate([jnp.zeros((B, 1, D_MODEL), jnp.float32), csum], axis=1)
    end = csum[:, POOL_BUF + 1:]
    pos = pos0 + jnp.arange(T)
    means = []
    for gi, w in enumerate(POOL_WINDOWS):
        sl = slice(gi * POOL_GROUP, (gi + 1) * POOL_GROUP)
        start = csum[:, POOL_BUF + 1 - w:POOL_BUF + 1 - w + T, sl]
        cnt = jnp.minimum(pos + 1, w).astype(jnp.float32)[None, :, None]
        means.append((end[..., sl] - start) / cnt)
    pooled = jnp.stack(means, axis=2)
    diff = pooled - h.astype(jnp.float32).reshape(B, T, N_POOL_GROUPS, POOL_GROUP)
    mixed = jnp.einsum('btgc,gcd->btgd', diff, w_grp).reshape(B, T, D_MODEL) * scale
    return mixed.astype(h.dtype), cat[:, -POOL_BUF:]


def sb_attend(q, k, v, qpos, kpos, bias):
    z = jnp.einsum('bqhd,bkhd->bhqk', q, k, preferred_element_type=jnp.float32) * (HEAD_DIM ** -0.5)
    z = z + bias.astype(jnp.float32)[None, :, None, None]
    valid = kpos[None, :] < qpos[:, None]
    log_keep = jnp.where(valid, jax.nn.log_sigmoid(-z), 0.0)
    between = lax.cumsum(log_keep, axis=3, reverse=True) - log_keep
    a = jnp.where(valid, jnp.exp(jax.nn.log_sigmoid(z) + between), 0.0)
    return jnp.einsum('bhqk,bkhd->bqhd', a, v.astype(jnp.float32))


def split_qkv(h, wqkv):
    B, T, _ = h.shape
    qkv = (h @ wqkv).reshape(B, T, 3, N_HEADS, HEAD_DIM)
    return qkv[:, :, 0], qkv[:, :, 1], qkv[:, :, 2]


def sb_prompt(h, wqkv, bias, wo):
    B, S, _ = h.shape
    q, k, v = split_qkv(h, wqkv)
    nb = S // Q_BLOCK
    qb = q.reshape(B, nb, Q_BLOCK, N_HEADS, HEAD_DIM).transpose(1, 0, 2, 3, 4)
    qpos = jnp.arange(S).reshape(nb, Q_BLOCK)
    kpos = jnp.arange(S)
    o = lax.map(lambda a: sb_attend(a[0], k, v, a[1], kpos, bias), (qb, qpos))
    o = o.transpose(1, 0, 2, 3, 4).reshape(B, S, D_MODEL).astype(h.dtype)
    return o @ wo, k, v


def sb_sample(h, k_past, v_past, wqkv, bias, wo):
    B, T, _ = h.shape
    q, k, v = split_qkv(h, wqkv)
    past = k_past.shape[1]
    k_all = jnp.concatenate([k_past.astype(k.dtype), k], axis=1)
    v_all = jnp.concatenate([v_past.astype(v.dtype), v], axis=1)
    qpos = past + jnp.arange(T)
    kpos = jnp.arange(past + T)
    o = sb_attend(q, k_all, v_all, qpos, kpos, bias).reshape(B, T, D_MODEL).astype(h.dtype)
    return o @ wo, k, v


def conv_ffn(h, buf, wg, wu, cw, cb, wd):
    T = h.shape[1]
    g = h @ wg
    u = h @ wu
    cat = jnp.concatenate([buf.astype(g.dtype), g], axis=1)
    c = cb + cw[0] * cat[:, 0:T]
    for tap in range(1, CONV_W):
        c = c + cw[tap] * cat[:, tap:tap + T]
    y = (jax.nn.gelu(c, approximate=True) * u) @ wd
    return y, cat[:, -(CONV_W - 1):]


def channel_and_ple(x, conv_buf, p_i, n_pre, n_post, wg, wu, cw, cb, wd, w_pg, w_pe, n_ple):
    f, new_buf = conv_ffn(rmsnorm(x, n_pre), conv_buf, wg, wu, cw, cb, wd)
    x = x + rmsnorm(f, n_post)
    e = p_i.astype(x.dtype) @ w_pe
    gate = jax.nn.sigmoid(x @ w_pg)
    x = x + rmsnorm(gate * e, n_ple)
    return x, new_buf


def setup_inputs(seed: int = 0) -> dict:
    key = jax.random.key(seed)
    ks = jax.random.split(key, 32)
    f32 = jnp.float32
    n_pages = PAST_LEN // PAGE_SIZE
    n_used = DEC_BATCH * n_pages
    n_phys = n_used + n_used // 4
    nrm = lambda k, shape, s=1.0: jax.random.normal(k, shape, f32) * s
    gain = lambda k, shape: 1.0 + 0.05 * jax.random.normal(k, shape, f32)
    page_table = jax.random.permutation(ks[0], n_phys)[:n_used].reshape(DEC_BATCH, n_pages).astype(jnp.int32)
    return {
        "x_prompt": nrm(ks[1], (BATCH, SEQ, D_MODEL)),
        "x_sample": nrm(ks[2], (DEC_BATCH, DEC_SEQ, D_MODEL)),
        "state_pool": nrm(ks[3], (DEC_BATCH, N_POOL_LAYERS, POOL_BUF, D_MODEL)),
        "state_conv": nrm(ks[4], (DEC_BATCH, DEPTH, CONV_W - 1, D_FF)),
        "cache_k": nrm(ks[5], (n_phys, N_SB_LAYERS, PAGE_SIZE, N_HEADS, HEAD_DIM)),
        "cache_v": nrm(ks[6], (n_phys, N_SB_LAYERS, PAGE_SIZE, N_HEADS, HEAD_DIM)),
        "page_table": page_table,
        "p_prompt": nrm(ks[7], (DEPTH, BATCH, SEQ, PLE_DIM)),
        "p_sample": nrm(ks[8], (DEPTH, DEC_BATCH, DEC_SEQ, PLE_DIM)),
        "norm_mix_pre": gain(ks[9], (DEPTH, D_MODEL)),
        "norm_mix_post": gain(ks[10], (DEPTH, D_MODEL)),
        "pool_w": nrm(ks[11], (N_POOL_LAYERS, N_POOL_GROUPS, POOL_GROUP, POOL_GROUP), POOL_GROUP ** -0.5),
        "pool_scale": 1.0 + 0.1 * jax.random.normal(ks[12], (N_POOL_LAYERS, D_MODEL), f32),
        "sb_wqkv": nrm(ks[13], (N_SB_LAYERS, D_MODEL, 3 * D_MODEL), D_MODEL ** -0.5),
        "sb_bias": SB_BIAS_INIT + 0.3 * jax.random.normal(ks[25], (N_SB_LAYERS, N_HEADS), f32),
        "sb_wo": nrm(ks[14], (N_SB_LAYERS, D_MODEL, D_MODEL), D_MODEL ** -0.5),
        "norm_ffn_pre": gain(ks[15], (DEPTH, D_MODEL)),
        "norm_ffn_post": gain(ks[16], (DEPTH, D_MODEL)),
        "ffn_w_gate": nrm(ks[17], (DEPTH, D_MODEL, D_FF), D_MODEL ** -0.5),
        "ffn_w_up": nrm(ks[18], (DEPTH, D_MODEL, D_FF), D_MODEL ** -0.5),
        "ffn_conv_w": nrm(ks[19], (DEPTH, CONV_W, D_FF), CONV_W ** -0.5),
        "ffn_conv_b": nrm(ks[20], (DEPTH, D_FF), 0.01),
        "ffn_w_down": nrm(ks[21], (DEPTH, D_FF, D_MODEL), D_FF ** -0.5),
        "ple_gate": nrm(ks[22], (DEPTH, D_MODEL, D_MODEL), D_MODEL ** -0.5),
        "ple_proj": nrm(ks[23], (DEPTH, PLE_DIM, D_MODEL), PLE_DIM ** -0.5),
        "ple_norm": gain(ks[24], (DEPTH, D_MODEL)),
    }


def reference(x_prompt, x_sample, state_pool, state_conv, cache_k, cache_v, page_table, p_prompt, p_sample,
              norm_mix_pre, norm_mix_post, pool_w, pool_scale, sb_wqkv, sb_bias, sb_wo,
              norm_ffn_pre, norm_ffn_post, ffn_w_gate, ffn_w_up, ffn_conv_w, ffn_conv_b, ffn_w_down,
              ple_gate, ple_proj, ple_norm):
    xp, xs = x_prompt, x_sample
    Bp, Bs = xp.shape[0], xs.shape[0]
    n_pages = page_table.shape[1]
    past = n_pages * PAGE_SIZE
    pool_p, pool_s, conv_p, conv_s = [], [], [], []
    kp_l, vp_l, ks_l, vs_l = [], [], [], []
    for i in range(DEPTH):
        j = i // N_MIXERS
        hp = rmsnorm(xp, norm_mix_pre[i])
        hs = rmsnorm(xs, norm_mix_pre[i])
        if i % N_MIXERS == 0:
            zero_buf = jnp.zeros((Bp, POOL_BUF, D_MODEL), hp.dtype)
            mp, bp = pool_mix(hp, zero_buf, 0, pool_w[j], pool_scale[j])
            ms, bs = pool_mix(hs, state_pool[:, j], past, pool_w[j], pool_scale[j])
            pool_p.append(bp)
            pool_s.append(bs)
        else:
            mp, kp, vp = sb_prompt(hp, sb_wqkv[j], sb_bias[j], sb_wo[j])
            k_past = cache_k[page_table, j].reshape(Bs, past, N_HEADS, HEAD_DIM)
            v_past = cache_v[page_table, j].reshape(Bs, past, N_HEADS, HEAD_DIM)
            ms, ksn, vsn = sb_sample(hs, k_past, v_past, sb_wqkv[j], sb_bias[j], sb_wo[j])
            kp_l.append(kp)
            vp_l.append(vp)
            ks_l.append(ksn)
            vs_l.append(vsn)
        xp = xp + rmsnorm(mp, norm_mix_post[i])
        xs = xs + rmsnorm(ms, norm_mix_post[i])
        ffn_args = (norm_ffn_pre[i], norm_ffn_post[i], ffn_w_gate[i], ffn_w_up[i], ffn_conv_w[i], ffn_conv_b[i],
                    ffn_w_down[i], ple_gate[i], ple_proj[i], ple_norm[i])
        zero_conv = jnp.zeros((Bp, CONV_W - 1, D_FF), xp.dtype)
        xp, cbp = channel_and_ple(xp, zero_conv, p_prompt[i], *ffn_args)
        xs, cbs = channel_and_ple(xs, state_conv[:, i], p_sample[i], *ffn_args)
        conv_p.append(cbp)
        conv_s.append(cbs)
    new_pool_prompt = jnp.stack(pool_p, axis=1)
    new_pool_sample = jnp.stack(pool_s, axis=1)
    new_conv_prompt = jnp.stack(conv_p, axis=1)
    new_conv_sample = jnp.stack(conv_s, axis=1)
    new_k_prompt = jnp.stack(kp_l, axis=1)
    new_v_prompt = jnp.stack(vp_l, axis=1)
    new_k_sample = jnp.stack(ks_l, axis=1)
    new_v_sample = jnp.stack(vs_l, axis=1)
    return (xp, xs, new_pool_prompt, new_pool_sample, new_conv_prompt, new_conv_sample,
            new_k_prompt, new_v_prompt, new_k_sample, new_v_sample)
```

```python
import functools
import math

import jax
import jax.numpy as jnp
from jax import lax
from jax.experimental import pallas as pl
from jax.experimental.pallas import tpu as pltpu

D_MODEL = 1024
N_HEADS = 16
HEAD_DIM = D_MODEL // N_HEADS
POOL_WINDOWS = (2, 4, 8, 16)
POOL_GROUP = D_MODEL // len(POOL_WINDOWS)
POOL_BUF = max(POOL_WINDOWS) - 1
CONV_W = 3
EPS = 1e-6

V7X_LANES = 128
V7X_SUBLANES = 8
V7X_VMEM_BYTES = 64 * 1024 * 1024

ROW_TILE = 512
FF_TILE = 256
POOL_TILE = 256
POOL_HALO = 128
ATT_BLOCK = 256
HEADS_PER_STEP = V7X_LANES // HEAD_DIM
PAGES_PER_STEP = 8
Q_ROWS = 2 * V7X_SUBLANES

_F32 = jnp.float32
_BF16 = jnp.bfloat16


def _cparams(semantics, vmem_bytes):
    assert vmem_bytes <= V7X_VMEM_BYTES
    return pltpu.CompilerParams(dimension_semantics=semantics, vmem_limit_bytes=int(vmem_bytes))


def _rms(x, g):
    return x * lax.rsqrt(jnp.mean(x * x, axis=-1, keepdims=True) + EPS) * g


def _split_bf16(x):
    hi = x.astype(_BF16)
    lo = (x - hi.astype(_F32)).astype(_BF16)
    return hi, lo


def _dot(a, b):
    return jnp.dot(a, b, preferred_element_type=_F32)


def _softplus(z):
    return jnp.maximum(z, 0.0) + jnp.log(1.0 + jnp.exp(-jnp.abs(z)))


def _pool_prompt_kernel(x_ref, halo_ref, band_ref, wgrp_ref, scale_ref, npre_ref, npost_ref,
                        out_ref, hlast_ref, mix_scr):
    i = pl.program_id(1)
    xt = x_ref[0]
    h = _rms(xt, npre_ref[...])
    hh = _rms(halo_ref[0], npre_ref[...])
    hh = jnp.where(i == 0, 0.0, hh)
    hext = jnp.concatenate([hh, h], axis=0)
    hi, lo = _split_bf16(hext)
    pos = i * POOL_TILE + lax.broadcasted_iota(jnp.int32, (POOL_TILE, 1), 0)
    for g, w in enumerate(POOL_WINDOWS):
        cols = slice(g * POOL_GROUP, (g + 1) * POOL_GROUP)
        band = band_ref[g]
        wsum = _dot(band, hi[:, cols]) + _dot(band, lo[:, cols])
        cnt = jnp.minimum(pos + 1, w).astype(_F32)
        diff = wsum / cnt - h[:, cols]
        mix_scr[:, cols] = _dot(diff.astype(_BF16), wgrp_ref[g])
    mixed = mix_scr[...] * scale_ref[...]
    out_ref[0] = xt + _rms(mixed, npost_ref[...])
    hlast_ref[0] = h[POOL_TILE - 2 * V7X_SUBLANES:, :]


def _pool_prompt(x, band, wgrp, scale, npre, npost):
    B, S, D = x.shape
    nt = S // POOL_TILE
    halo_per_tile = POOL_TILE // POOL_HALO
    keep = 2 * V7X_SUBLANES
    vec = pl.BlockSpec((1, D), lambda b, i: (0, 0))
    return pl.pallas_call(
        _pool_prompt_kernel,
        grid=(B, nt),
        in_specs=[
            pl.BlockSpec((1, POOL_TILE, D), lambda b, i: (b, i, 0)),
            pl.BlockSpec((1, POOL_HALO, D), lambda b, i: (b, jnp.maximum(i * halo_per_tile - 1, 0), 0)),
            pl.BlockSpec(band.shape, lambda b, i: (0, 0, 0)),
            pl.BlockSpec(wgrp.shape, lambda b, i: (0, 0, 0)),
            vec, vec, vec,
        ],
        out_specs=[
            pl.BlockSpec((1, POOL_TILE, D), lambda b, i: (b, i, 0)),
            pl.BlockSpec((1, keep, D), lambda b, i: (b, i, 0)),
        ],
        out_shape=[jax.ShapeDtypeStruct((B, S, D), _F32),
                   jax.ShapeDtypeStruct((B, nt * keep, D), _F32)],
        scratch_shapes=[pltpu.VMEM((POOL_TILE, D), _F32)],
        compiler_params=_cparams(("arbitrary", "arbitrary"), 32 * 2**20),
        name="pool_prompt",
    )(x, x, band, wgrp, scale, npre, npost)


def _pool_sample_kernel(x_ref, buf_ref, wgrp_ref, scale_ref, npre_ref, npost_ref,
                        out_ref, h_ref, mix_scr, *, past):
    xt = x_ref[...]
    h = _rms(xt, npre_ref[...])
    row = lax.broadcasted_iota(jnp.int32, (1, POOL_BUF, 1), 1)
    for g, w in enumerate(POOL_WINDOWS):
        cols = slice(g * POOL_GROUP, (g + 1) * POOL_GROUP)
        in_window = (row >= POOL_BUF - (w - 1)).astype(_F32)
        wsum = jnp.sum(buf_ref[:, :, cols] * in_window, axis=1) + h[:, cols]
        diff = wsum / float(min(past + 1, w)) - h[:, cols]
        mix_scr[:, cols] = _dot(diff.astype(_BF16), wgrp_ref[g])
    mixed = mix_scr[...] * scale_ref[...]
    out_ref[...] = xt + _rms(mixed, npost_ref[...])
    h_ref[...] = h


def _pool_sample(x, buf, wgrp, scale, npre, npost, past):
    M, D = x.shape
    tb = 32
    vec = pl.BlockSpec((1, D), lambda i: (0, 0))
    return pl.pallas_call(
        functools.partial(_pool_sample_kernel, past=past),
        grid=(M // tb,),
        in_specs=[
            pl.BlockSpec((tb, D), lambda i: (i, 0)),
            pl.BlockSpec((tb, POOL_BUF, D), lambda i: (i, 0, 0)),
            pl.BlockSpec(wgrp.shape, lambda i: (0, 0, 0)),
            vec, vec, vec,
        ],
        out_specs=[pl.BlockSpec((tb, D), lambda i: (i, 0)), pl.BlockSpec((tb, D), lambda i: (i, 0))],
        out_shape=[jax.ShapeDtypeStruct((M, D), _F32), jax.ShapeDtypeStruct((M, D), _F32)],
        scratch_shapes=[pltpu.VMEM((tb, D), _F32)],
        compiler_params=_cparams(("arbitrary",), 32 * 2**20),
        name="pool_sample",
    )(x, buf, wgrp, scale, npre, npost)


def _qkv_kernel(x_ref, npre_ref, w_ref, q_ref, k_ref, v_ref, kb_ref, vb_ref):
    hb = _rms(x_ref[...], npre_ref[...]).astype(_BF16)
    D = D_MODEL
    q = _dot(hb, w_ref[:, 0:D])
    q_ref[...] = (q * (HEAD_DIM ** -0.5)).astype(_BF16)
    k = _dot(hb, w_ref[:, D:2 * D])
    k_ref[...] = k
    kb_ref[...] = k.astype(_BF16)
    v = _dot(hb, w_ref[:, 2 * D:3 * D])
    v_ref[...] = v
    vb_ref[...] = v.astype(_BF16)


def _qkv(x, npre, w, tm):
    M, D = x.shape
    row = lambda dt: (pl.BlockSpec((tm, D), lambda i: (i, 0)), jax.ShapeDtypeStruct((M, D), dt))
    outs = [row(_BF16), row(_F32), row(_F32), row(_BF16), row(_BF16)]
    return pl.pallas_call(
        _qkv_kernel,
        grid=(M // tm,),
        in_specs=[pl.BlockSpec((tm, D), lambda i: (i, 0)),
                  pl.BlockSpec((1, D), lambda i: (0, 0)),
                  pl.BlockSpec(w.shape, lambda i: (0, 0))],
        out_specs=[o[0] for o in outs],
        out_shape=[o[1] for o in outs],
        compiler_params=_cparams(("arbitrary",), 48 * 2**20),
        name="rms_qkv",
    )(x, npre, w)


def _attn_block(qh, k_blk, v_blk, tri, bias, carry, diagonal):
    z = lax.dot_general(qh, k_blk, (((1,), (1,)), ((), ())), preferred_element_type=_F32) + bias
    sp = _softplus(z)
    log_keep = -sp
    if diagonal:
        n = ATT_BLOCK
        valid = (lax.broadcasted_iota(jnp.int32, (n, n), 1) < lax.broadcasted_iota(jnp.int32, (n, n), 0))
        log_keep = jnp.where(valid, log_keep, 0.0)
    hi, lo = _split_bf16(log_keep)
    between = _dot(hi, tri) + _dot(lo, tri) + carry
    a = jnp.exp(z - sp + between)
    if diagonal:
        a = jnp.where(valid, a, 0.0)
    return _dot(a.astype(_BF16), v_blk), jnp.sum(log_keep, axis=-1, keepdims=True)


def _attn_prompt_kernel(bias_ref, q_ref, k_ref, v_ref, tri_ref, o_ref, acc_scr, carry_scr):
    hp = pl.program_id(1)
    qi = pl.program_id(2)
    n = ATT_BLOCK
    q = q_ref[0]
    tri = tri_ref[...]
    lane = lax.broadcasted_iota(jnp.int32, (1, V7X_LANES), 1)
    out = jnp.zeros((n, V7X_LANES), _F32)
    for hh in range(HEADS_PER_STEP):
        in_head = (lane >= hh * HEAD_DIM) & (lane < (hh + 1) * HEAD_DIM)
        qh = jnp.where(in_head, q, jnp.zeros_like(q))
        bias = bias_ref[hp * HEADS_PER_STEP + hh]
        start = pl.multiple_of(qi * n, n)
        pv, rs = _attn_block(qh, k_ref[0, pl.ds(start, n), :], v_ref[0, pl.ds(start, n), :],
                             tri, bias, 0.0, True)
        acc_scr[...] = pv
        carry_scr[...] = rs

        def body(t, _):
            j = qi - 1 - t
            st = pl.multiple_of(j * n, n)
            pv_j, rs_j = _attn_block(qh, k_ref[0, pl.ds(st, n), :], v_ref[0, pl.ds(st, n), :],
                                     tri, bias, carry_scr[...], False)
            acc_scr[...] += pv_j
            carry_scr[...] += rs_j
            return 0

        lax.fori_loop(0, qi, body, 0)
        out = jnp.where(in_head, acc_scr[...], out)
    o_ref[0] = out.astype(o_ref.dtype)


def _attn_prompt(q, kb, vb, tri, bias):
    B, S, D = q.shape
    n = ATT_BLOCK
    return pl.pallas_call(
        _attn_prompt_kernel,
        grid_spec=pltpu.PrefetchScalarGridSpec(
            num_scalar_prefetch=1,
            grid=(B, D // V7X_LANES, S // n),
            in_specs=[
                pl.BlockSpec((1, n, V7X_LANES), lambda b, hp, qi, bias: (b, qi, hp)),
                pl.BlockSpec((1, S, V7X_LANES), lambda b, hp, qi, bias: (b, 0, hp)),
                pl.BlockSpec((1, S, V7X_LANES), lambda b, hp, qi, bias: (b, 0, hp)),
                pl.BlockSpec((n, n), lambda b, hp, qi, bias: (0, 0)),
            ],
            out_specs=pl.BlockSpec((1, n, V7X_LANES), lambda b, hp, qi, bias: (b, qi, hp)),
            scratch_shapes=[pltpu.VMEM((n, V7X_LANES), _F32), pltpu.VMEM((n, 1), _F32)],
        ),
        out_shape=jax.ShapeDtypeStruct((B, S, D), _BF16),
        compiler_params=_cparams(("arbitrary", "arbitrary", "arbitrary"), 32 * 2**20),
        name="sb_attn_prompt",
    )(bias, q, kb, vb, tri)


def _attn_sample_kernel(pt_ref, bias_ref, q_ref, tri_ref, *refs):
    del pt_ref
    P = PAGES_PER_STEP
    R = Q_ROWS
    k_refs, v_refs = refs[:P], refs[P:2 * P]
    o_ref, z_scr, carry_scr = refs[2 * P:]
    step = pl.program_id(1)

    @pl.when(step == 0)
    def _():
        o_ref[...] = jnp.zeros_like(o_ref)
        carry_scr[...] = jnp.zeros_like(carry_scr)

    tri = tri_ref[...]
    for s in range(P):
        for h in range(N_HEADS):
            rows = slice(h * R, (h + 1) * R)
            z_scr[rows, :] = _dot(q_ref[0, rows, :], k_refs[s][h].astype(_BF16))
        z = z_scr[...] + bias_ref[...]
        sp = _softplus(z)
        log_keep = -sp
        hi, lo = _split_bf16(log_keep)
        between = _dot(hi, tri) + _dot(lo, tri) + carry_scr[...]
        a = jnp.exp(z - sp + between).astype(_BF16)
        carry_scr[...] += jnp.sum(log_keep, axis=-1, keepdims=True)
        for h in range(N_HEADS):
            rows = slice(h * R, (h + 1) * R)
            o_ref[0, rows, :] += lax.dot_general(a[rows, :], v_refs[s][h].astype(_BF16),
                                                 (((1,), (1,)), ((), ())), preferred_element_type=_F32)


def _attn_sample(q_rep, cache_kt, cache_vt, page_table, tri, bias_rep, layer):
    M, rows, _ = q_rep.shape
    n_pages = page_table.shape[1]
    page = cache_kt.shape[-1]
    P = PAGES_PER_STEP
    assert n_pages % P == 0 and page == V7X_LANES and rows == N_HEADS * Q_ROWS

    def page_spec(s):
        def index(b, i, pt):
            return (pt[b, n_pages - 1 - (i * P + s)], layer, 0, 0, 0)
        return pl.BlockSpec((None, None, N_HEADS, HEAD_DIM, page), index)

    specs = [page_spec(s) for s in range(P)]
    return pl.pallas_call(
        _attn_sample_kernel,
        grid_spec=pltpu.PrefetchScalarGridSpec(
            num_scalar_prefetch=1,
            grid=(M, n_pages // P),
            in_specs=[
                pl.BlockSpec((rows, 1), lambda b, i, pt: (0, 0)),
                pl.BlockSpec((1, rows, HEAD_DIM), lambda b, i, pt: (b, 0, 0)),
                pl.BlockSpec((page, page), lambda b, i, pt: (0, 0)),
            ] + specs + specs,
            out_specs=pl.BlockSpec((1, rows, HEAD_DIM), lambda b, i, pt: (b, 0, 0)),
            scratch_shapes=[pltpu.VMEM((rows, page), _F32), pltpu.VMEM((rows, 1), _F32)],
        ),
        out_shape=jax.ShapeDtypeStruct((M, rows, HEAD_DIM), _F32),
        compiler_params=_cparams(("arbitrary", "arbitrary"), 48 * 2**20),
        name="sb_attn_sample",
    )(page_table, bias_rep, q_rep, tri, *([cache_kt] * P), *([cache_vt] * P))


def _proj_res_kernel(x_ref, o_ref, w_ref, npost_ref, out_ref):
    m = _dot(o_ref[...], w_ref[...])
    out_ref[...] = x_ref[...] + _rms(m, npost_ref[...])


def _proj_res(x, o, w, npost, tm):
    M, D = x.shape
    return pl.pallas_call(
        _proj_res_kernel,
        grid=(M // tm,),
        in_specs=[pl.BlockSpec((tm, D), lambda i: (i, 0)),
                  pl.BlockSpec((tm, D), lambda i: (i, 0)),
                  pl.BlockSpec(w.shape, lambda i: (0, 0)),
                  pl.BlockSpec((1, D), lambda i: (0, 0))],
        out_specs=pl.BlockSpec((tm, D), lambda i: (i, 0)),
        out_shape=jax.ShapeDtypeStruct((M, D), _F32),
        compiler_params=_cparams(("arbitrary",), 32 * 2**20),
        name="proj_res",
    )(x, o, w, npost)


def _gelu_tanh(c):
    return 0.5 * c * (1.0 + jnp.tanh(math.sqrt(2.0 / math.pi) * (c + 0.044715 * (c * c * c))))


def _ffn_kernel(*refs, tm, tiles_per_seq, n_ff_tiles):
    seq_mode = tiles_per_seq > 0
    (x_ref, p_ref, npre_ref, npost_ref, nple_ref, wg_ref, wu_ref, cw_ref, cb_ref, wd_ref,
     wpg_ref, wpe_ref) = refs[:12]
    if seq_mode:
        out_ref, gout_ref, hb_scr, acc_scr, carry_scr = refs[12:]
    else:
        s0_ref, s1_ref, out_ref, gout_ref, hb_scr, acc_scr = refs[12:]
    i = pl.program_id(0)
    f = pl.program_id(1)

    @pl.when(f == 0)
    def _():
        hb_scr[...] = _rms(x_ref[...], npre_ref[...]).astype(_BF16)
        acc_scr[...] = jnp.zeros_like(acc_scr)

    hb = hb_scr[...]
    g = _dot(hb, wg_ref[...])
    u = _dot(hb, wu_ref[...])
    if seq_mode:
        sub = V7X_SUBLANES
        tail = g[tm - sub:, :]
        prev = jnp.where(i % tiles_per_seq == 0, 0.0, carry_scr[f])
        carry_scr[f] = tail
        gout_ref[...] = tail
        row = lax.broadcasted_iota(jnp.int32, (sub, FF_TILE), 0)
        g1 = pltpu.roll(g, 1, 0)
        g2 = pltpu.roll(g, 2, 0)
        g1 = jnp.concatenate([jnp.where(row < 1, pltpu.roll(prev, 1, 0), g1[:sub]), g1[sub:]], axis=0)
        g2 = jnp.concatenate([jnp.where(row < 2, pltpu.roll(prev, 2, 0), g2[:sub]), g2[sub:]], axis=0)
    else:
        g2 = s0_ref[...]
        g1 = s1_ref[...]
        gout_ref[...] = g
    c = cb_ref[...] + cw_ref[0:1, :] * g2 + cw_ref[1:2, :] * g1 + cw_ref[2:3, :] * g
    act = (_gelu_tanh(c) * u).astype(_BF16)
    acc_scr[...] += _dot(act, wd_ref[...])

    @pl.when(f == n_ff_tiles - 1)
    def _():
        x1 = x_ref[...] + _rms(acc_scr[...], npost_ref[...])
        e = _dot(p_ref[...].astype(_BF16), wpe_ref[...])
        gate = 1.0 / (1.0 + jnp.exp(-_dot(x1.astype(_BF16), wpg_ref[...])))
        out_ref[...] = x1 + _rms(gate * e, nple_ref[...])


def _ffn(x, p, npre, npost, nple, wg, wu, cw, cb, wd, wpg, wpe, *, tm, tiles_per_seq, prev_rows=None):
    M, D = x.shape
    F = wg.shape[1]
    nf = F // FF_TILE
    nt = M // tm
    seq_mode = tiles_per_seq > 0
    vec = pl.BlockSpec((1, D), lambda i, f: (0, 0))
    in_specs = [
        pl.BlockSpec((tm, D), lambda i, f: (i, 0)),
        pl.BlockSpec((tm, p.shape[1]), lambda i, f: (i, 0)),
        vec, vec, vec,
        pl.BlockSpec((D, FF_TILE), lambda i, f: (0, f)),
        pl.BlockSpec((D, FF_TILE), lambda i, f: (0, f)),
        pl.BlockSpec((CONV_W, FF_TILE), lambda i, f: (0, f)),
        pl.BlockSpec((1, FF_TILE), lambda i, f: (0, f)),
        pl.BlockSpec((FF_TILE, D), lambda i, f: (f, 0)),
        pl.BlockSpec(wpg.shape, lambda i, f: (0, 0)),
        pl.BlockSpec(wpe.shape, lambda i, f: (0, 0)),
    ]
    args = [x, p, npre, npost, nple, wg, wu, cw, cb, wd, wpg, wpe]
    scratch = [pltpu.VMEM((tm, D), _BF16), pltpu.VMEM((tm, D), _F32)]
    if seq_mode:
        g_rows = V7X_SUBLANES
        scratch.append(pltpu.VMEM((nf, V7X_SUBLANES, FF_TILE), _F32))
    else:
        g_rows = tm
        in_specs += [pl.BlockSpec((tm, FF_TILE), lambda i, f: (i, f))] * 2
        args += list(prev_rows)
    return pl.pallas_call(
        functools.partial(_ffn_kernel, tm=tm, tiles_per_seq=tiles_per_seq, n_ff_tiles=nf),
        grid=(nt, nf),
        in_specs=in_specs,
        out_specs=[pl.BlockSpec((tm, D), lambda i, f: (i, 0)),
                   pl.BlockSpec((g_rows, FF_TILE), lambda i, f: (i, f))],
        out_shape=[jax.ShapeDtypeStruct((M, D), _F32),
                   jax.ShapeDtypeStruct((nt * g_rows, F), _F32)],
        scratch_shapes=scratch,
        compiler_params=_cparams(("arbitrary", "arbitrary"), 48 * 2**20),
        name="conv_ffn_ple",
    )(*args)


def _pool_band():
    t = jnp.arange(POOL_TILE)[:, None] + POOL_HALO
    c = jnp.arange(POOL_HALO + POOL_TILE)[None, :]
    return jnp.stack([((c <= t) & (c > t - w)).astype(_BF16) for w in POOL_WINDOWS])


def _later_keys(n):
    return (jnp.arange(n)[:, None] > jnp.arange(n)[None, :]).astype(_BF16)


def kernel(x_prompt, x_sample, state_pool, state_conv, cache_k, cache_v, page_table, p_prompt, p_sample, norm_mix_pre, norm_mix_post, pool_w, pool_scale, sb_wqkv, sb_bias, sb_wo, norm_ffn_pre, norm_ffn_post, ffn_w_gate, ffn_w_up, ffn_conv_w, ffn_conv_b, ffn_w_down, ple_gate, ple_proj, ple_norm):
    Bp, S, D = x_prompt.shape
    Bs = x_sample.shape[0]
    depth = norm_mix_pre.shape[0]
    d_ff = ffn_w_gate.shape[2]
    past = page_table.shape[1] * cache_k.shape[2]
    Mp = Bp * S
    assert x_sample.shape[1] == 1 and S % ROW_TILE == 0 and S % POOL_TILE == 0 and S % ATT_BLOCK == 0
    assert d_ff % FF_TILE == 0 and D == D_MODEL

    xp = x_prompt
    xs = x_sample.reshape(Bs, D)
    vec = lambda a: a.reshape(1, -1)
    band = _pool_band()
    tri_p = _later_keys(ATT_BLOCK)
    tri_s = _later_keys(cache_k.shape[2])
    cache_kt = jnp.transpose(cache_k, (0, 1, 3, 4, 2))
    cache_vt = jnp.transpose(cache_v, (0, 1, 3, 4, 2))

    pool_p, pool_s, conv_p, conv_s = [], [], [], []
    kp_l, vp_l, ks_l, vs_l = [], [], [], []
    for i in range(depth):
        j = i // 2
        npre, npost = vec(norm_mix_pre[i]), vec(norm_mix_post[i])
        if i % 2 == 0:
            wgrp = pool_w[j].astype(_BF16)
            scale = vec(pool_scale[j])
            xp, hlast = _pool_prompt(xp, band, wgrp, scale, npre, npost)
            pool_p.append(hlast[:, -POOL_BUF:, :])
            buf = state_pool[:, j]
            xs, hs = _pool_sample(xs, buf, wgrp, scale, npre, npost, past)
            pool_s.append(jnp.concatenate([buf[:, 1:], hs[:, None, :]], axis=1))
        else:
            wqkv = sb_wqkv[j].astype(_BF16)
            wo = sb_wo[j].astype(_BF16)
            q, k, v, kb, vb = _qkv(xp.reshape(Mp, D), npre, wqkv, ROW_TILE)
            o = _attn_prompt(q.reshape(Bp, S, D), kb.reshape(Bp, S, D), vb.reshape(Bp, S, D), tri_p, sb_bias[j])
            xp = _proj_res(xp.reshape(Mp, D), o.reshape(Mp, D), wo, npost, ROW_TILE).reshape(Bp, S, D)
            kp_l.append(k.reshape(Bp, S, N_HEADS, HEAD_DIM))
            vp_l.append(v.reshape(Bp, S, N_HEADS, HEAD_DIM))

            qs, ksn, vsn, _, _ = _qkv(xs, npre, wqkv, Bs)
            q_rep = jnp.broadcast_to(qs.reshape(Bs, N_HEADS, 1, HEAD_DIM), (Bs, N_HEADS, Q_ROWS, HEAD_DIM))
            bias_rep = jnp.repeat(sb_bias[j], Q_ROWS).reshape(N_HEADS * Q_ROWS, 1)
            o_rep = _attn_sample(q_rep.reshape(Bs, N_HEADS * Q_ROWS, HEAD_DIM), cache_kt, cache_vt, page_table,
                                 tri_s, bias_rep, j)
            os_ = o_rep.reshape(Bs, N_HEADS, Q_ROWS, HEAD_DIM)[:, :, 0, :].reshape(Bs, D).astype(_BF16)
            xs = _proj_res(xs, os_, wo, npost, Bs)
            ks_l.append(ksn.reshape(Bs, 1, N_HEADS, HEAD_DIM))
            vs_l.append(vsn.reshape(Bs, 1, N_HEADS, HEAD_DIM))

        ffn_w = (vec(norm_ffn_pre[i]), vec(norm_ffn_post[i]), vec(ple_norm[i]),
                 ffn_w_gate[i].astype(_BF16), ffn_w_up[i].astype(_BF16), ffn_conv_w[i],
                 vec(ffn_conv_b[i]), ffn_w_down[i].astype(_BF16),
                 ple_gate[i].astype(_BF16), ple_proj[i].astype(_BF16))
        xp2, gtail = _ffn(xp.reshape(Mp, D), p_prompt[i].reshape(Mp, -1), *ffn_w,
                          tm=ROW_TILE, tiles_per_seq=S // ROW_TILE)
        xp = xp2.reshape(Bp, S, D)
        gtail = gtail.reshape(Bp, S // ROW_TILE, V7X_SUBLANES, d_ff)
        conv_p.append(gtail[:, -1, -(CONV_W - 1):, :])
        sc = state_conv[:, i]
        xs, gs = _ffn(xs, p_sample[i].reshape(Bs, -1), *ffn_w, tm=Bs, tiles_per_seq=0,
                      prev_rows=(sc[:, 0], sc[:, 1]))
        conv_s.append(jnp.stack([sc[:, 1], gs], axis=1))

    return (xp, xs.reshape(Bs, 1, D),
            jnp.stack(pool_p, axis=1), jnp.stack(pool_s, axis=1),
            jnp.stack(conv_p, axis=1), jnp.stack(conv_s, axis=1),
            jnp.stack(kp_l, axis=1), jnp.stack(vp_l, axis=1),
            jnp.stack(ks_l, axis=1), jnp.stack(vs_l, axis=1))
```

```python
import functools
import math

import jax
import jax.numpy as jnp
from jax import lax
from jax.experimental import pallas as pl
from jax.experimental.pallas import tpu as pltpu

D_MODEL = 1024
N_HEADS = 16
HEAD_DIM = D_MODEL // N_HEADS
POOL_WINDOWS = (2, 4, 8, 16)
POOL_GROUP = D_MODEL // len(POOL_WINDOWS)
POOL_BUF = max(POOL_WINDOWS) - 1
CONV_W = 3
EPS = 1e-6

V7X_LANES = 128
V7X_SUBLANES = 8
V7X_VMEM_BYTES = 64 * 1024 * 1024

ROW_TILE = 512
FF_TILE = 1408
POOL_TILE = 256
POOL_HALO = 128
ATT_BLOCK = 256
HEADS_PER_STEP = V7X_LANES // HEAD_DIM
PAGES_PER_STEP = 8

_F32 = jnp.float32
_BF16 = jnp.bfloat16


def _cparams(semantics, vmem_bytes):
    assert vmem_bytes <= V7X_VMEM_BYTES
    return pltpu.CompilerParams(dimension_semantics=semantics, vmem_limit_bytes=int(vmem_bytes))


def _rms(x, g):
    return x * lax.rsqrt(jnp.mean(x * x, axis=-1, keepdims=True) + EPS) * g


def _split_bf16(x):
    hi = x.astype(_BF16)
    lo = (x - hi.astype(_F32)).astype(_BF16)
    return hi, lo


def _dot(a, b):
    return jnp.dot(a, b, preferred_element_type=_F32)


def _softplus(z):
    return jnp.maximum(z, 0.0) + jnp.log(1.0 + jnp.exp(-jnp.abs(z)))


def _pool_prompt_kernel(x_ref, halo_ref, band_ref, wgrp_ref, scale_ref, npre_ref, npost_ref,
                        out_ref, hlast_ref, mix_scr):
    i = pl.program_id(1)
    xt = x_ref[0]
    h = _rms(xt, npre_ref[...])
    hh = _rms(halo_ref[0], npre_ref[...])
    hh = jnp.where(i == 0, 0.0, hh)
    hext = jnp.concatenate([hh, h], axis=0)
    hi, lo = _split_bf16(hext)
    pos = i * POOL_TILE + lax.broadcasted_iota(jnp.int32, (POOL_TILE, 1), 0)
    for g, w in enumerate(POOL_WINDOWS):
        cols = slice(g * POOL_GROUP, (g + 1) * POOL_GROUP)
        band = band_ref[g]
        wsum = _dot(band, hi[:, cols]) + _dot(band, lo[:, cols])
        cnt = jnp.minimum(pos + 1, w).astype(_F32)
        diff = wsum / cnt - h[:, cols]
        mix_scr[:, cols] = _dot(diff.astype(_BF16), wgrp_ref[g])
    mixed = mix_scr[...] * scale_ref[...]
    out_ref[0] = xt + _rms(mixed, npost_ref[...])
    hlast_ref[0] = h[POOL_TILE - 2 * V7X_SUBLANES:, :]


def _pool_prompt(x, band, wgrp, scale, npre, npost):
    B, S, D = x.shape
    nt = S // POOL_TILE
    halo_per_tile = POOL_TILE // POOL_HALO
    keep = 2 * V7X_SUBLANES
    vec = pl.BlockSpec((1, D), lambda b, i: (0, 0))
    return pl.pallas_call(
        _pool_prompt_kernel,
        grid=(B, nt),
        in_specs=[
            pl.BlockSpec((1, POOL_TILE, D), lambda b, i: (b, i, 0)),
            pl.BlockSpec((1, POOL_HALO, D), lambda b, i: (b, jnp.maximum(i * halo_per_tile - 1, 0), 0)),
            pl.BlockSpec(band.shape, lambda b, i: (0, 0, 0)),
            pl.BlockSpec(wgrp.shape, lambda b, i: (0, 0, 0)),
            vec, vec, vec,
        ],
        out_specs=[
            pl.BlockSpec((1, POOL_TILE, D), lambda b, i: (b, i, 0)),
            pl.BlockSpec((1, keep, D), lambda b, i: (b, i, 0)),
        ],
        out_shape=[jax.ShapeDtypeStruct((B, S, D), _F32),
                   jax.ShapeDtypeStruct((B, nt * keep, D), _F32)],
        scratch_shapes=[pltpu.VMEM((POOL_TILE, D), _F32)],
        compiler_params=_cparams(("arbitrary", "arbitrary"), 32 * 2**20),
        name="pool_prompt",
    )(x, x, band, wgrp, scale, npre, npost)


def _pool_sample_kernel(x_ref, buf_ref, wgrp_ref, scale_ref, npre_ref, npost_ref,
                        out_ref, h_ref, mix_scr, *, past):
    xt = x_ref[...]
    h = _rms(xt, npre_ref[...])
    row = lax.broadcasted_iota(jnp.int32, (1, POOL_BUF, 1), 1)
    for g, w in enumerate(POOL_WINDOWS):
        cols = slice(g * POOL_GROUP, (g + 1) * POOL_GROUP)
        in_window = (row >= POOL_BUF - (w - 1)).astype(_F32)
        wsum = jnp.sum(buf_ref[:, :, cols] * in_window, axis=1) + h[:, cols]
        diff = wsum / float(min(past + 1, w)) - h[:, cols]
        mix_scr[:, cols] = _dot(diff.astype(_BF16), wgrp_ref[g])
    mixed = mix_scr[...] * scale_ref[...]
    out_ref[...] = xt + _rms(mixed, npost_ref[...])
    h_ref[...] = h


def _pool_sample(x, buf, wgrp, scale, npre, npost, past):
    M, D = x.shape
    tb = 32
    vec = pl.BlockSpec((1, D), lambda i: (0, 0))
    return pl.pallas_call(
        functools.partial(_pool_sample_kernel, past=past),
        grid=(M // tb,),
        in_specs=[
            pl.BlockSpec((tb, D), lambda i: (i, 0)),
            pl.BlockSpec((tb, POOL_BUF, D), lambda i: (i, 0, 0)),
            pl.BlockSpec(wgrp.shape, lambda i: (0, 0, 0)),
            vec, vec, vec,
        ],
        out_specs=[pl.BlockSpec((tb, D), lambda i: (i, 0)), pl.BlockSpec((tb, D), lambda i: (i, 0))],
        out_shape=[jax.ShapeDtypeStruct((M, D), _F32), jax.ShapeDtypeStruct((M, D), _F32)],
        scratch_shapes=[pltpu.VMEM((tb, D), _F32)],
        compiler_params=_cparams(("arbitrary",), 32 * 2**20),
        name="pool_sample",
    )(x, buf, wgrp, scale, npre, npost)


def _qkv_kernel(x_ref, npre_ref, w_ref, q_ref, k_ref, v_ref, kb_ref, vb_ref):
    hb = _rms(x_ref[...], npre_ref[...]).astype(_BF16)
    D = D_MODEL
    q = _dot(hb, w_ref[:, 0:D])
    q_ref[...] = (q * (HEAD_DIM ** -0.5)).astype(_BF16)
    k = _dot(hb, w_ref[:, D:2 * D])
    k_ref[...] = k
    kb_ref[...] = k.astype(_BF16)
    v = _dot(hb, w_ref[:, 2 * D:3 * D])
    v_ref[...] = v
    vb_ref[...] = v.astype(_BF16)


def _qkv(x, npre, w, tm):
    M, D = x.shape
    row = lambda dt: (pl.BlockSpec((tm, D), lambda i: (i, 0)), jax.ShapeDtypeStruct((M, D), dt))
    outs = [row(_BF16), row(_F32), row(_F32), row(_BF16), row(_BF16)]
    return pl.pallas_call(
        _qkv_kernel,
        grid=(M // tm,),
        in_specs=[pl.BlockSpec((tm, D), lambda i: (i, 0)),
                  pl.BlockSpec((1, D), lambda i: (0, 0)),
                  pl.BlockSpec(w.shape, lambda i: (0, 0))],
        out_specs=[o[0] for o in outs],
        out_shape=[o[1] for o in outs],
        compiler_params=_cparams(("arbitrary",), 48 * 2**20),
        name="rms_qkv",
    )(x, npre, w)


def _attn_block(qh, k_blk, v_blk, ntri_ref, bias, carry, diagonal):
    z = lax.dot_general(qh, k_blk, (((1,), (1,)), ((), ())), preferred_element_type=_F32) + bias
    sp = _softplus(z)
    if diagonal:
        n = ATT_BLOCK
        valid = (lax.broadcasted_iota(jnp.int32, (n, n), 1) < lax.broadcasted_iota(jnp.int32, (n, n), 0))
        sp = jnp.where(valid, sp, 0.0)
    hi, lo = _split_bf16(sp)
    a = jnp.exp(z + _dot(jnp.concatenate([hi, lo], axis=1), ntri_ref[...]) + carry)
    if diagonal:
        a = jnp.where(valid, a, 0.0)
    return _dot(a.astype(_BF16), v_blk), jnp.sum(sp, axis=-1, keepdims=True)


def _attn_prompt_kernel(bias_ref, q_ref, k_ref, v_ref, ntri_ref, o_ref, qm_scr, acc_scr, carry_scr):
    hp = pl.program_id(1)
    qi = pl.program_id(2)
    n = ATT_BLOCK
    heads = range(HEADS_PER_STEP)
    q = q_ref[0]
    lane = lax.broadcasted_iota(jnp.int32, (1, V7X_LANES), 1)
    in_head = [(lane >= hh * HEAD_DIM) & (lane < (hh + 1) * HEAD_DIM) for hh in heads]
    for hh in heads:
        qm_scr[hh] = jnp.where(in_head[hh], q, jnp.zeros_like(q))
    bias = [bias_ref[hp * HEADS_PER_STEP + hh] for hh in heads]

    def kv(j):
        st = pl.multiple_of(j * n, n)
        return k_ref[0, pl.ds(st, n), :], v_ref[0, pl.ds(st, n), :]

    def two_blocks(kv0, kv1, first_is_diagonal):
        for hh in heads:
            c0 = 0.0 if first_is_diagonal else carry_scr[hh]
            pv0, rs0 = _attn_block(qm_scr[hh], *kv0, ntri_ref, bias[hh], c0, first_is_diagonal)
            c1 = c0 - rs0
            pv1, rs1 = _attn_block(qm_scr[hh], *kv1, ntri_ref, bias[hh], c1, False)
            if first_is_diagonal:
                acc_scr[hh] = pv0 + pv1
            else:
                acc_scr[hh] += pv0 + pv1
            carry_scr[hh] = c1 - rs1

    @pl.when(qi == 0)
    def _():
        for hh in heads:
            acc_scr[hh], _ = _attn_block(qm_scr[hh], *kv(0), ntri_ref, bias[hh], 0.0, True)

    @pl.when(qi > 0)
    def _():
        two_blocks(kv(qi), kv(qi - 1), True)

    rest = jnp.maximum(qi - 1, 0)

    def pair(t, _):
        j0 = qi - 2 - 2 * t
        two_blocks(kv(j0), kv(j0 - 1), False)
        return 0

    lax.fori_loop(0, jnp.right_shift(rest, 1), pair, 0)

    @pl.when(jnp.bitwise_and(rest, 1) == 1)
    def _():
        k0, v0 = kv(0)
        for hh in heads:
            pv, _ = _attn_block(qm_scr[hh], k0, v0, ntri_ref, bias[hh], carry_scr[hh], False)
            acc_scr[hh] += pv

    out = acc_scr[0]
    for hh in heads[1:]:
        out = jnp.where(in_head[hh], acc_scr[hh], out)
    o_ref[0] = out.astype(o_ref.dtype)


def _attn_prompt(q, kb, vb, ntri, bias):
    B, S, D = q.shape
    n = ATT_BLOCK
    H = HEADS_PER_STEP
    return pl.pallas_call(
        _attn_prompt_kernel,
        grid_spec=pltpu.PrefetchScalarGridSpec(
            num_scalar_prefetch=1,
            grid=(B, D // V7X_LANES, S // n),
            in_specs=[
                pl.BlockSpec((1, n, V7X_LANES), lambda b, hp, qi, bias: (b, qi, hp)),
                pl.BlockSpec((1, S, V7X_LANES), lambda b, hp, qi, bias: (b, 0, hp)),
                pl.BlockSpec((1, S, V7X_LANES), lambda b, hp, qi, bias: (b, 0, hp)),
                pl.BlockSpec(ntri.shape, lambda b, hp, qi, bias: (0, 0)),
            ],
            out_specs=pl.BlockSpec((1, n, V7X_LANES), lambda b, hp, qi, bias: (b, qi, hp)),
            scratch_shapes=[pltpu.VMEM((H, n, V7X_LANES), _BF16), pltpu.VMEM((H, n, V7X_LANES), _F32),
                            pltpu.VMEM((H, n, 1), _F32)],
        ),
        out_shape=jax.ShapeDtypeStruct((B, S, D), _BF16),
        compiler_params=_cparams(("arbitrary", "arbitrary", "arbitrary"), 32 * 2**20),
        name="sb_attn_prompt",
    )(bias, q, kb, vb, ntri)


def _attn_sample_kernel(pt_ref, bias_ref, q_ref, ntri_ref, *refs):
    del pt_ref
    P = PAGES_PER_STEP
    k_refs, v_refs = refs[:P], refs[P:2 * P]
    o_ref, acc_scr, carry_scr = refs[2 * P:]
    step = pl.program_id(1)
    page = V7X_LANES

    @pl.when(step == 0)
    def _():
        acc_scr[...] = jnp.zeros_like(acc_scr)
        carry_scr[...] = jnp.zeros_like(carry_scr)

    def pages(page_refs):
        return jnp.concatenate([r[...].reshape(D_MODEL, page).astype(_BF16) for r in page_refs], axis=1)

    z = _dot(q_ref[0], pages(k_refs)) + bias_ref[...]
    sp = _softplus(z)
    hi, lo = _split_bf16(sp)
    carry = carry_scr[...]
    between = []
    for s in range(P):
        cols = slice(s * page, (s + 1) * page)
        between.append(_dot(jnp.concatenate([hi[:, cols], lo[:, cols]], axis=1), ntri_ref[...]) + carry)
        carry = carry - jnp.sum(sp[:, cols], axis=-1, keepdims=True)
    carry_scr[...] = carry
    a = jnp.exp(z - sp + jnp.concatenate(between, axis=1)).astype(_BF16)
    acc_scr[...] += lax.dot_general(a, pages(v_refs), (((1,), (1,)), ((), ())), preferred_element_type=_F32)

    @pl.when(step == pl.num_programs(1) - 1)
    def _():
        head = lax.broadcasted_iota(jnp.int32, (N_HEADS, D_MODEL), 0)
        col_head = lax.broadcasted_iota(jnp.int32, (N_HEADS, D_MODEL), 1) // HEAD_DIM
        o_ref[0] = jnp.sum(jnp.where(head == col_head, acc_scr[...], 0.0), axis=0, keepdims=True)


def _attn_sample(q_bd, cache_kt, cache_vt, page_table, ntri, bias_col, layer):
    M, rows, D = q_bd.shape
    n_pages = page_table.shape[1]
    page = cache_kt.shape[-1]
    P = PAGES_PER_STEP
    assert n_pages % P == 0 and page == V7X_LANES and rows == N_HEADS and D == D_MODEL

    def page_spec(s):
        def index(b, i, pt):
            return (pt[b, n_pages - 1 - (i * P + s)], layer, 0, 0, 0)
        return pl.BlockSpec((None, None, N_HEADS, HEAD_DIM, page), index)

    specs = [page_spec(s) for s in range(P)]
    return pl.pallas_call(
        _attn_sample_kernel,
        grid_spec=pltpu.PrefetchScalarGridSpec(
            num_scalar_prefetch=1,
            grid=(M, n_pages // P),
            in_specs=[
                pl.BlockSpec((rows, 1), lambda b, i, pt: (0, 0)),
                pl.BlockSpec((1, rows, D), lambda b, i, pt: (b, 0, 0)),
                pl.BlockSpec(ntri.shape, lambda b, i, pt: (0, 0)),
            ] + specs + specs,
            out_specs=pl.BlockSpec((1, 1, D), lambda b, i, pt: (b, 0, 0)),
            scratch_shapes=[pltpu.VMEM((rows, D), _F32), pltpu.VMEM((rows, 1), _F32)],
        ),
        out_shape=jax.ShapeDtypeStruct((M, 1, D), _F32),
        compiler_params=_cparams(("arbitrary", "arbitrary"), 56 * 2**20),
        name="sb_attn_sample",
    )(page_table, bias_col, q_bd, ntri, *([cache_kt] * P), *([cache_vt] * P))


def _proj_res_kernel(x_ref, o_ref, w_ref, npost_ref, out_ref):
    m = _dot(o_ref[...], w_ref[...])
    out_ref[...] = x_ref[...] + _rms(m, npost_ref[...])


def _proj_res(x, o, w, npost, tm):
    M, D = x.shape
    return pl.pallas_call(
        _proj_res_kernel,
        grid=(M // tm,),
        in_specs=[pl.BlockSpec((tm, D), lambda i: (i, 0)),
                  pl.BlockSpec((tm, D), lambda i: (i, 0)),
                  pl.BlockSpec(w.shape, lambda i: (0, 0)),
                  pl.BlockSpec((1, D), lambda i: (0, 0))],
        out_specs=pl.BlockSpec((tm, D), lambda i: (i, 0)),
        out_shape=jax.ShapeDtypeStruct((M, D), _F32),
        compiler_params=_cparams(("arbitrary",), 32 * 2**20),
        name="proj_res",
    )(x, o, w, npost)


def _gelu_tanh(c):
    return 0.5 * c * (1.0 + jnp.tanh(math.sqrt(2.0 / math.pi) * (c + 0.044715 * (c * c * c))))


def _ffn_kernel(*refs, tm, tiles_per_seq, n_ff_tiles):
    seq_mode = tiles_per_seq > 0
    nf = n_ff_tiles
    (x_ref, p_ref, npre_ref, npost_ref, nple_ref, wg_ref, wu_ref, cw_ref, cb_ref, wd_ref,
     wpg_ref, wpe_ref) = refs[:12]
    if seq_mode:
        out_ref, gout_ref, hb_scr, acc_scr, carry_scr = refs[12:]
    else:
        s0_ref, s1_ref, out_ref, gout_ref, hb_scr, acc_scr = refs[12:]
    i = pl.program_id(0)
    f = pl.program_id(1)
    sub = V7X_SUBLANES

    @pl.when(f == 0)
    def _():
        hb_scr[...] = _rms(x_ref[...], npre_ref[...]).astype(_BF16)
        acc_scr[...] = jnp.zeros_like(acc_scr)

    hb = hb_scr[...]
    g = _dot(hb, wg_ref[...])
    u = _dot(hb, wu_ref[...])
    if seq_mode:
        tail = g[tm - sub:, :]
        prev = jnp.where(i % tiles_per_seq == 0, 0.0, carry_scr[f])
        carry_scr[f] = tail
        gout_ref[...] = tail
        row = lax.broadcasted_iota(jnp.int32, (sub, FF_TILE), 0)
        g1 = pltpu.roll(g, 1, 0)
        g2 = pltpu.roll(g, 2, 0)
        g1 = jnp.concatenate([jnp.where(row < 1, pltpu.roll(prev, 1, 0), g1[:sub]), g1[sub:]], axis=0)
        g2 = jnp.concatenate([jnp.where(row < 2, pltpu.roll(prev, 2, 0), g2[:sub]), g2[sub:]], axis=0)
    else:
        g2 = s0_ref[...]
        g1 = s1_ref[...]
        gout_ref[...] = g
    c = cb_ref[...] + cw_ref[0:1, :] * g2 + cw_ref[1:2, :] * g1 + cw_ref[2:3, :] * g
    act = (_gelu_tanh(c) * u).astype(_BF16)
    acc_scr[...] += _dot(act, wd_ref[...])

    @pl.when(f == nf - 1)
    def _():
        e = _dot(p_ref[...].astype(_BF16), wpe_ref[...])
        x1 = x_ref[...] + _rms(acc_scr[...], npost_ref[...])
        gate = 1.0 / (1.0 + jnp.exp(-_dot(x1.astype(_BF16), wpg_ref[...])))
        out_ref[...] = x1 + _rms(gate * e, nple_ref[...])


def _ffn(x, p, npre, npost, nple, wg, wu, cw, cb, wd, wpg, wpe, *, tm, tiles_per_seq, prev_rows=None):
    M, D = x.shape
    F = wg.shape[1]
    nf = F // FF_TILE
    nt = M // tm
    seq_mode = tiles_per_seq > 0
    vec = pl.BlockSpec((1, D), lambda i, f: (0, 0))
    in_specs = [
        pl.BlockSpec((tm, D), lambda i, f: (i, 0)),
        pl.BlockSpec((tm, p.shape[1]), lambda i, f: (i, 0)),
        vec, vec, vec,
        pl.BlockSpec((D, FF_TILE), lambda i, f: (0, f)),
        pl.BlockSpec((D, FF_TILE), lambda i, f: (0, f)),
        pl.BlockSpec((CONV_W, FF_TILE), lambda i, f: (0, f)),
        pl.BlockSpec((1, FF_TILE), lambda i, f: (0, f)),
        pl.BlockSpec((FF_TILE, D), lambda i, f: (f, 0)),
        pl.BlockSpec(wpg.shape, lambda i, f: (0, 0)),
        pl.BlockSpec(wpe.shape, lambda i, f: (0, 0)),
    ]
    args = [x, p, npre, npost, nple, wg, wu, cw, cb, wd, wpg, wpe]
    scratch = [pltpu.VMEM((tm, D), _BF16), pltpu.VMEM((tm, D), _F32)]
    if seq_mode:
        g_rows = V7X_SUBLANES
        scratch.append(pltpu.VMEM((nf, V7X_SUBLANES, FF_TILE), _F32))
    else:
        g_rows = tm
        in_specs += [pl.BlockSpec((tm, FF_TILE), lambda i, f: (i, f))] * 2
        args += list(prev_rows)
    return pl.pallas_call(
        functools.partial(_ffn_kernel, tm=tm, tiles_per_seq=tiles_per_seq, n_ff_tiles=nf),
        grid=(nt, nf),
        in_specs=in_specs,
        out_specs=[pl.BlockSpec((tm, D), lambda i, f: (i, 0)),
                   pl.BlockSpec((g_rows, FF_TILE), lambda i, f: (i, f))],
        out_shape=[jax.ShapeDtypeStruct((M, D), _F32),
                   jax.ShapeDtypeStruct((nt * g_rows, F), _F32)],
        scratch_shapes=scratch,
        compiler_params=_cparams(("arbitrary", "arbitrary"), 56 * 2**20),
        name="conv_ffn_ple",
    )(*args)


def _pool_band():
    t = jnp.arange(POOL_TILE)[:, None] + POOL_HALO
    c = jnp.arange(POOL_HALO + POOL_TILE)[None, :]
    return jnp.stack([((c <= t) & (c > t - w)).astype(_BF16) for w in POOL_WINDOWS])


def _later_keys_neg(n, inclusive):
    jp, j = jnp.arange(n)[:, None], jnp.arange(n)[None, :]
    later = (jp >= j) if inclusive else (jp > j)
    return jnp.tile(-later.astype(_BF16), (2, 1))


def kernel(x_prompt, x_sample, state_pool, state_conv, cache_k, cache_v, page_table, p_prompt, p_sample, norm_mix_pre, norm_mix_post, pool_w, pool_scale, sb_wqkv, sb_bias, sb_wo, norm_ffn_pre, norm_ffn_post, ffn_w_gate, ffn_w_up, ffn_conv_w, ffn_conv_b, ffn_w_down, ple_gate, ple_proj, ple_norm):
    Bp, S, D = x_prompt.shape
    Bs = x_sample.shape[0]
    depth = norm_mix_pre.shape[0]
    d_ff = ffn_w_gate.shape[2]
    past = page_table.shape[1] * cache_k.shape[2]
    Mp = Bp * S
    assert x_sample.shape[1] == 1 and S % ROW_TILE == 0 and S % POOL_TILE == 0 and S % ATT_BLOCK == 0
    assert d_ff % FF_TILE == 0 and D == D_MODEL

    xp = x_prompt
    xs = x_sample.reshape(Bs, D)
    vec = lambda a: a.reshape(1, -1)
    band = _pool_band()
    tri_p = _later_keys_neg(ATT_BLOCK, True)
    tri_s = _later_keys_neg(cache_k.shape[2], False)
    head_of_col = (jnp.arange(D) // HEAD_DIM)[None, :]
    cache_kt = jnp.transpose(cache_k, (0, 1, 3, 4, 2))
    cache_vt = jnp.transpose(cache_v, (0, 1, 3, 4, 2))

    pool_p, pool_s, conv_p, conv_s = [], [], [], []
    kp_l, vp_l, ks_l, vs_l = [], [], [], []
    for i in range(depth):
        j = i // 2
        npre, npost = vec(norm_mix_pre[i]), vec(norm_mix_post[i])
        if i % 2 == 0:
            wgrp = pool_w[j].astype(_BF16)
            scale = vec(pool_scale[j])
            xp, hlast = _pool_prompt(xp, band, wgrp, scale, npre, npost)
            pool_p.append(hlast[:, -POOL_BUF:, :])
            buf = state_pool[:, j]
            xs, hs = _pool_sample(xs, buf, wgrp, scale, npre, npost, past)
            pool_s.append(jnp.concatenate([buf[:, 1:], hs[:, None, :]], axis=1))
        else:
            wqkv = sb_wqkv[j].astype(_BF16)
            wo = sb_wo[j].astype(_BF16)
            q, k, v, kb, vb = _qkv(xp.reshape(Mp, D), npre, wqkv, ROW_TILE)
            o = _attn_prompt(q.reshape(Bp, S, D), kb.reshape(Bp, S, D), vb.reshape(Bp, S, D), tri_p, sb_bias[j])
            xp = _proj_res(xp.reshape(Mp, D), o.reshape(Mp, D), wo, npost, ROW_TILE).reshape(Bp, S, D)
            kp_l.append(k.reshape(Bp, S, N_HEADS, HEAD_DIM))
            vp_l.append(v.reshape(Bp, S, N_HEADS, HEAD_DIM))

            qs, ksn, vsn, _, _ = _qkv(xs, npre, wqkv, Bs)
            q_bd = jnp.where(head_of_col[None] == jnp.arange(N_HEADS)[None, :, None], qs[:, None, :], 0)
            os_ = _attn_sample(q_bd, cache_kt, cache_vt, page_table, tri_s, sb_bias[j].reshape(N_HEADS, 1), j)
            xs = _proj_res(xs, os_.reshape(Bs, D).astype(_BF16), wo, npost, Bs)
            ks_l.append(ksn.reshape(Bs, 1, N_HEADS, HEAD_DIM))
            vs_l.append(vsn.reshape(Bs, 1, N_HEADS, HEAD_DIM))

        ffn_w = (vec(norm_ffn_pre[i]), vec(norm_ffn_post[i]), vec(ple_norm[i]),
                 ffn_w_gate[i].astype(_BF16), ffn_w_up[i].astype(_BF16), ffn_conv_w[i],
                 vec(ffn_conv_b[i]), ffn_w_down[i].astype(_BF16),
                 ple_gate[i].astype(_BF16), ple_proj[i].astype(_BF16))
        xp2, gtail = _ffn(xp.reshape(Mp, D), p_prompt[i].reshape(Mp, -1), *ffn_w,
                          tm=ROW_TILE, tiles_per_seq=S // ROW_TILE)
        xp = xp2.reshape(Bp, S, D)
        gtail = gtail.reshape(Bp, S // ROW_TILE, V7X_SUBLANES, d_ff)
        conv_p.append(gtail[:, -1, -(CONV_W - 1):, :])
        sc = state_conv[:, i]
        xs, gs = _ffn(xs, p_sample[i].reshape(Bs, -1), *ffn_w, tm=Bs, tiles_per_seq=0,
                      prev_rows=(sc[:, 0], sc[:, 1]))
        conv_s.append(jnp.stack([sc[:, 1], gs], axis=1))

    return (xp, xs.reshape(Bs, 1, D),
            jnp.stack(pool_p, axis=1), jnp.stack(pool_s, axis=1),
            jnp.stack(conv_p, axis=1), jnp.stack(conv_s, axis=1),
            jnp.stack(kp_l, axis=1), jnp.stack(vp_l, axis=1),
            jnp.stack(ks_l, axis=1), jnp.stack(vs_l, axis=1))
```

```python
import functools
import math

import jax
import jax.numpy as jnp
from jax import lax
from jax.experimental import pallas as pl
from jax.experimental.pallas import tpu as pltpu

D_MODEL = 1024
N_HEADS = 16
HEAD_DIM = D_MODEL // N_HEADS
POOL_WINDOWS = (2, 4, 8, 16)
POOL_GROUP = D_MODEL // len(POOL_WINDOWS)
POOL_BUF = max(POOL_WINDOWS) - 1
CONV_W = 3
EPS = 1e-6

V7X_LANES = 128
V7X_SUBLANES = 8
V7X_VMEM_BYTES = 64 * 1024 * 1024

ROW_TILE = 512
FF_TILE = 1408
POOL_TILE = 256
POOL_HALO = 128
ATT_BLOCK = 256
HEADS_PER_STEP = V7X_LANES // HEAD_DIM
PAGES_PER_STEP = 8

_F32 = jnp.float32
_BF16 = jnp.bfloat16


def _cparams(semantics, vmem_bytes):
    assert vmem_bytes <= V7X_VMEM_BYTES
    return pltpu.CompilerParams(dimension_semantics=semantics, vmem_limit_bytes=int(vmem_bytes))


def _rms(x, g):
    return x * lax.rsqrt(jnp.mean(x * x, axis=-1, keepdims=True) + EPS) * g


def _split_bf16(x):
    hi = x.astype(_BF16)
    lo = (x - hi.astype(_F32)).astype(_BF16)
    return hi, lo


def _dot(a, b):
    return jnp.dot(a, b, preferred_element_type=_F32)


def _softplus(z):
    return jnp.maximum(z, 0.0) + jnp.log(1.0 + jnp.exp(-jnp.abs(z)))


def _pool_prompt_kernel(x_ref, halo_ref, band_ref, wgrp_ref, scale_ref, npre_ref, npost_ref,
                        out_ref, hlast_ref, mix_scr):
    i = pl.program_id(1)
    xt = x_ref[0]
    h = _rms(xt, npre_ref[...])
    hh = _rms(halo_ref[0], npre_ref[...])
    hh = jnp.where(i == 0, 0.0, hh)
    hext = jnp.concatenate([hh, h], axis=0)
    hi, lo = _split_bf16(hext)
    pos = i * POOL_TILE + lax.broadcasted_iota(jnp.int32, (POOL_TILE, 1), 0)
    for g, w in enumerate(POOL_WINDOWS):
        cols = slice(g * POOL_GROUP, (g + 1) * POOL_GROUP)
        band = band_ref[g]
        wsum = _dot(band, hi[:, cols]) + _dot(band, lo[:, cols])
        cnt = jnp.minimum(pos + 1, w).astype(_F32)
        diff = wsum / cnt - h[:, cols]
        mix_scr[:, cols] = _dot(diff.astype(_BF16), wgrp_ref[g])
    mixed = mix_scr[...] * scale_ref[...]
    out_ref[0] = xt + _rms(mixed, npost_ref[...])
    hlast_ref[0] = h[POOL_TILE - 2 * V7X_SUBLANES:, :]


def _pool_prompt(x, band, wgrp, scale, npre, npost):
    B, S, D = x.shape
    nt = S // POOL_TILE
    halo_per_tile = POOL_TILE // POOL_HALO
    keep = 2 * V7X_SUBLANES
    vec = pl.BlockSpec((1, D), lambda b, i: (0, 0))
    return pl.pallas_call(
        _pool_prompt_kernel,
        grid=(B, nt),
        in_specs=[
            pl.BlockSpec((1, POOL_TILE, D), lambda b, i: (b, i, 0)),
            pl.BlockSpec((1, POOL_HALO, D), lambda b, i: (b, jnp.maximum(i * halo_per_tile - 1, 0), 0)),
            pl.BlockSpec(band.shape, lambda b, i: (0, 0, 0)),
            pl.BlockSpec(wgrp.shape, lambda b, i: (0, 0, 0)),
            vec, vec, vec,
        ],
        out_specs=[
            pl.BlockSpec((1, POOL_TILE, D), lambda b, i: (b, i, 0)),
            pl.BlockSpec((1, keep, D), lambda b, i: (b, i, 0)),
        ],
        out_shape=[jax.ShapeDtypeStruct((B, S, D), _F32),
                   jax.ShapeDtypeStruct((B, nt * keep, D), _F32)],
        scratch_shapes=[pltpu.VMEM((POOL_TILE, D), _F32)],
        compiler_params=_cparams(("arbitrary", "arbitrary"), 32 * 2**20),
        name="pool_prompt",
    )(x, x, band, wgrp, scale, npre, npost)


def _pool_sample_kernel(x_ref, buf_ref, wgrp_ref, scale_ref, npre_ref, npost_ref,
                        out_ref, h_ref, mix_scr, *, past):
    xt = x_ref[...]
    h = _rms(xt, npre_ref[...])
    row = lax.broadcasted_iota(jnp.int32, (1, POOL_BUF, 1), 1)
    for g, w in enumerate(POOL_WINDOWS):
        cols = slice(g * POOL_GROUP, (g + 1) * POOL_GROUP)
        in_window = (row >= POOL_BUF - (w - 1)).astype(_F32)
        wsum = jnp.sum(buf_ref[:, :, cols] * in_window, axis=1) + h[:, cols]
        diff = wsum / float(min(past + 1, w)) - h[:, cols]
        mix_scr[:, cols] = _dot(diff.astype(_BF16), wgrp_ref[g])
    mixed = mix_scr[...] * scale_ref[...]
    out_ref[...] = xt + _rms(mixed, npost_ref[...])
    h_ref[...] = h


def _pool_sample(x, buf, wgrp, scale, npre, npost, past):
    M, D = x.shape
    tb = 32
    vec = pl.BlockSpec((1, D), lambda i: (0, 0))
    return pl.pallas_call(
        functools.partial(_pool_sample_kernel, past=past),
        grid=(M // tb,),
        in_specs=[
            pl.BlockSpec((tb, D), lambda i: (i, 0)),
            pl.BlockSpec((tb, POOL_BUF, D), lambda i: (i, 0, 0)),
            pl.BlockSpec(wgrp.shape, lambda i: (0, 0, 0)),
            vec, vec, vec,
        ],
        out_specs=[pl.BlockSpec((tb, D), lambda i: (i, 0)), pl.BlockSpec((tb, D), lambda i: (i, 0))],
        out_shape=[jax.ShapeDtypeStruct((M, D), _F32), jax.ShapeDtypeStruct((M, D), _F32)],
        scratch_shapes=[pltpu.VMEM((tb, D), _F32)],
        compiler_params=_cparams(("arbitrary",), 32 * 2**20),
        name="pool_sample",
    )(x, buf, wgrp, scale, npre, npost)


def _qkv_kernel(*refs, seq_major_kv):
    x_ref, npre_ref, w_ref = refs[:3]
    q_ref, k_ref, v_ref = refs[-5:-2] if seq_major_kv else refs[-3:]
    hb = _rms(x_ref[...], npre_ref[...]).astype(_BF16)
    D = D_MODEL
    q = _dot(hb, w_ref[:, 0:D])
    q_ref[...] = (q * (HEAD_DIM ** -0.5)).astype(_BF16)
    for j, out_ref in enumerate((k_ref, v_ref)):
        kv = _dot(hb, w_ref[:, (j + 1) * D:(j + 2) * D])
        if seq_major_kv:
            refs[-2 + j][...] = kv.astype(_BF16)
            out_ref[...] = kv.T.reshape(N_HEADS, HEAD_DIM, kv.shape[0])
        else:
            out_ref[...] = kv


def _qkv(x, npre, w, tm, leaves=None):
    M, D = x.shape
    row = lambda dt: (pl.BlockSpec((tm, D), lambda i: (i, 0)), jax.ShapeDtypeStruct((M, D), dt))
    in_specs = [pl.BlockSpec((tm, D), lambda i: (i, 0)),
                pl.BlockSpec((1, D), lambda i: (0, 0)),
                pl.BlockSpec(w.shape, lambda i: (0, 0))]
    args = [x, npre, w]
    aliases = {}
    if leaves is None:
        outs = [row(_BF16), row(_F32), row(_F32)]
    else:
        seq_len, n_layers, layer, earlier = leaves
        tps = seq_len // tm
        leaf = (pl.BlockSpec((None, None, N_HEADS, HEAD_DIM, tm), lambda i: (i // tps, layer, 0, 0, i % tps)),
                jax.ShapeDtypeStruct((M // seq_len, n_layers, N_HEADS, HEAD_DIM, seq_len), _F32))
        outs = [row(_BF16), leaf, leaf, row(_BF16), row(_BF16)]
        if earlier is not None:
            in_specs += [pl.BlockSpec(memory_space=pl.ANY)] * 2
            args += list(earlier)
            aliases = {3: 1, 4: 2}
    return pl.pallas_call(
        functools.partial(_qkv_kernel, seq_major_kv=leaves is not None),
        grid=(M // tm,),
        in_specs=in_specs,
        out_specs=[o[0] for o in outs],
        out_shape=[o[1] for o in outs],
        input_output_aliases=aliases,
        compiler_params=_cparams(("arbitrary",), 48 * 2**20),
        name="rms_qkv",
    )(*args)


def _attn_block(qh, k_blk, v_blk, ntri_ref, bias, carry, diagonal):
    z = lax.dot_general(qh, k_blk, (((1,), (1,)), ((), ())), preferred_element_type=_F32) + bias
    sp = _softplus(z)
    log_beta = z - sp
    if diagonal:
        n = ATT_BLOCK
        valid = (lax.broadcasted_iota(jnp.int32, (n, n), 1) < lax.broadcasted_iota(jnp.int32, (n, n), 0))
        sp = jnp.where(valid, sp, 0.0)
    a = jnp.exp(log_beta + _dot(sp.astype(_BF16), ntri_ref[...]) + carry)
    if diagonal:
        a = jnp.where(valid, a, 0.0)
    return _dot(a.astype(_BF16), v_blk), jnp.sum(sp, axis=-1, keepdims=True)


def _attn_prompt_kernel(bias_ref, q_ref, k_ref, v_ref, ntri_ref, o_ref, qm_scr, acc_scr, carry_scr):
    hp = pl.program_id(1)
    qi = pl.program_id(2)
    n = ATT_BLOCK
    heads = range(HEADS_PER_STEP)
    q = q_ref[0]
    lane = lax.broadcasted_iota(jnp.int32, (1, V7X_LANES), 1)
    in_head = [(lane >= hh * HEAD_DIM) & (lane < (hh + 1) * HEAD_DIM) for hh in heads]
    for hh in heads:
        qm_scr[hh] = jnp.where(in_head[hh], q, jnp.zeros_like(q))
    bias = [bias_ref[hp * HEADS_PER_STEP + hh] for hh in heads]

    def kv(j):
        st = pl.multiple_of(j * n, n)
        return k_ref[0, pl.ds(st, n), :], v_ref[0, pl.ds(st, n), :]

    def two_blocks(kv0, kv1, first_is_diagonal):
        for hh in heads:
            c0 = 0.0 if first_is_diagonal else carry_scr[hh]
            pv0, rs0 = _attn_block(qm_scr[hh], *kv0, ntri_ref, bias[hh], c0, first_is_diagonal)
            c1 = c0 - rs0
            pv1, rs1 = _attn_block(qm_scr[hh], *kv1, ntri_ref, bias[hh], c1, False)
            if first_is_diagonal:
                acc_scr[hh] = pv0 + pv1
            else:
                acc_scr[hh] += pv0 + pv1
            carry_scr[hh] = c1 - rs1

    @pl.when(qi == 0)
    def _():
        for hh in heads:
            acc_scr[hh], _ = _attn_block(qm_scr[hh], *kv(0), ntri_ref, bias[hh], 0.0, True)

    @pl.when(qi > 0)
    def _():
        two_blocks(kv(qi), kv(qi - 1), True)

    rest = jnp.maximum(qi - 1, 0)

    def pair(t, _):
        j0 = qi - 2 - 2 * t
        two_blocks(kv(j0), kv(j0 - 1), False)
        return 0

    lax.fori_loop(0, jnp.right_shift(rest, 1), pair, 0)

    @pl.when(jnp.bitwise_and(rest, 1) == 1)
    def _():
        k0, v0 = kv(0)
        for hh in heads:
            pv, _ = _attn_block(qm_scr[hh], k0, v0, ntri_ref, bias[hh], carry_scr[hh], False)
            acc_scr[hh] += pv

    out = acc_scr[0]
    for hh in heads[1:]:
        out = jnp.where(in_head[hh], acc_scr[hh], out)
    o_ref[0] = out.astype(o_ref.dtype)


def _attn_prompt(q, kb, vb, ntri, bias):
    B, S, D = q.shape
    n = ATT_BLOCK
    H = HEADS_PER_STEP
    return pl.pallas_call(
        _attn_prompt_kernel,
        grid_spec=pltpu.PrefetchScalarGridSpec(
            num_scalar_prefetch=1,
            grid=(B, D // V7X_LANES, S // n),
            in_specs=[
                pl.BlockSpec((1, n, V7X_LANES), lambda b, hp, qi, bias: (b, qi, hp)),
                pl.BlockSpec((1, S, V7X_LANES), lambda b, hp, qi, bias: (b, 0, hp)),
                pl.BlockSpec((1, S, V7X_LANES), lambda b, hp, qi, bias: (b, 0, hp)),
                pl.BlockSpec(ntri.shape, lambda b, hp, qi, bias: (0, 0)),
            ],
            out_specs=pl.BlockSpec((1, n, V7X_LANES), lambda b, hp, qi, bias: (b, qi, hp)),
            scratch_shapes=[pltpu.VMEM((H, n, V7X_LANES), _BF16), pltpu.VMEM((H, n, V7X_LANES), _F32),
                            pltpu.VMEM((H, n, 1), _F32)],
        ),
        out_shape=jax.ShapeDtypeStruct((B, S, D), _BF16),
        compiler_params=_cparams(("arbitrary", "arbitrary", "arbitrary"), 32 * 2**20),
        name="sb_attn_prompt",
    )(bias, q, kb, vb, ntri)


def _attn_sample_kernel(pt_ref, bias_ref, q_ref, ntri_ref, *refs):
    del pt_ref
    P = PAGES_PER_STEP
    k_refs, v_refs = refs[:P], refs[P:2 * P]
    o_ref, acc_scr, carry_scr = refs[2 * P:]
    step = pl.program_id(1)
    page = V7X_LANES

    @pl.when(step == 0)
    def _():
        acc_scr[...] = jnp.zeros_like(acc_scr)
        carry_scr[...] = jnp.zeros_like(carry_scr)

    def pages(page_refs):
        return jnp.concatenate([r[...].reshape(D_MODEL, page).astype(_BF16) for r in page_refs], axis=1)

    z = _dot(q_ref[0], pages(k_refs)) + bias_ref[...]
    sp = _softplus(z)
    hi, lo = _split_bf16(sp)
    carry = carry_scr[...]
    between = []
    for s in range(P):
        cols = slice(s * page, (s + 1) * page)
        between.append(_dot(jnp.concatenate([hi[:, cols], lo[:, cols]], axis=1), ntri_ref[...]) + carry)
        carry = carry - jnp.sum(sp[:, cols], axis=-1, keepdims=True)
    carry_scr[...] = carry
    a = jnp.exp(z - sp + jnp.concatenate(between, axis=1)).astype(_BF16)
    acc_scr[...] += lax.dot_general(a, pages(v_refs), (((1,), (1,)), ((), ())), preferred_element_type=_F32)

    @pl.when(step == pl.num_programs(1) - 1)
    def _():
        head = lax.broadcasted_iota(jnp.int32, (N_HEADS, D_MODEL), 0)
        col_head = lax.broadcasted_iota(jnp.int32, (N_HEADS, D_MODEL), 1) // HEAD_DIM
        o_ref[0] = jnp.sum(jnp.where(head == col_head, acc_scr[...], 0.0), axis=0, keepdims=True)


def _attn_sample(q_bd, cache_kt, cache_vt, page_table, ntri, bias_col, layer):
    M, rows, D = q_bd.shape
    n_pages = page_table.shape[1]
    page = cache_kt.shape[-1]
    P = PAGES_PER_STEP
    assert n_pages % P == 0 and page == V7X_LANES and rows == N_HEADS and D == D_MODEL

    def page_spec(s):
        def index(b, i, pt):
            return (pt[b, n_pages - 1 - (i * P + s)], layer, 0, 0, 0)
        return pl.BlockSpec((None, None, N_HEADS, HEAD_DIM, page), index)

    specs = [page_spec(s) for s in range(P)]
    return pl.pallas_call(
        _attn_sample_kernel,
        grid_spec=pltpu.PrefetchScalarGridSpec(
            num_scalar_prefetch=1,
            grid=(M, n_pages // P),
            in_specs=[
                pl.BlockSpec((rows, 1), lambda b, i, pt: (0, 0)),
                pl.BlockSpec((1, rows, D), lambda b, i, pt: (b, 0, 0)),
                pl.BlockSpec(ntri.shape, lambda b, i, pt: (0, 0)),
            ] + specs + specs,
            out_specs=pl.BlockSpec((1, 1, D), lambda b, i, pt: (b, 0, 0)),
            scratch_shapes=[pltpu.VMEM((rows, D), _F32), pltpu.VMEM((rows, 1), _F32)],
        ),
        out_shape=jax.ShapeDtypeStruct((M, 1, D), _F32),
        compiler_params=_cparams(("arbitrary", "arbitrary"), 56 * 2**20),
        name="sb_attn_sample",
    )(page_table, bias_col, q_bd, ntri, *([cache_kt] * P), *([cache_vt] * P))


def _proj_res_kernel(x_ref, o_ref, w_ref, npost_ref, out_ref):
    m = _dot(o_ref[...], w_ref[...])
    out_ref[...] = x_ref[...] + _rms(m, npost_ref[...])


def _proj_res(x, o, w, npost, tm):
    M, D = x.shape
    return pl.pallas_call(
        _proj_res_kernel,
        grid=(M // tm,),
        in_specs=[pl.BlockSpec((tm, D), lambda i: (i, 0)),
                  pl.BlockSpec((tm, D), lambda i: (i, 0)),
                  pl.BlockSpec(w.shape, lambda i: (0, 0)),
                  pl.BlockSpec((1, D), lambda i: (0, 0))],
        out_specs=pl.BlockSpec((tm, D), lambda i: (i, 0)),
        out_shape=jax.ShapeDtypeStruct((M, D), _F32),
        compiler_params=_cparams(("arbitrary",), 32 * 2**20),
        name="proj_res",
    )(x, o, w, npost)


def _gelu_tanh(c):
    return 0.5 * c * (1.0 + jnp.tanh(math.sqrt(2.0 / math.pi) * (c + 0.044715 * (c * c * c))))


def _ffn_kernel(*refs, tm, tiles_per_seq, n_ff_tiles):
    seq_mode = tiles_per_seq > 0
    nf = n_ff_tiles
    (x_ref, p_ref, npre_ref, npost_ref, nple_ref, wg_ref, wu_ref, cw_ref, cb_ref, wd_ref,
     wpg_ref, wpe_ref) = refs[:12]
    if seq_mode:
        out_ref, gout_ref, hb_scr, acc_scr, carry_scr = refs[12:]
    else:
        s0_ref, s1_ref, out_ref, gout_ref, hb_scr, acc_scr = refs[12:]
    i = pl.program_id(0)
    f = pl.program_id(1)
    sub = V7X_SUBLANES

    @pl.when(f == 0)
    def _():
        hb_scr[...] = _rms(x_ref[...], npre_ref[...]).astype(_BF16)
        acc_scr[...] = jnp.zeros_like(acc_scr)

    hb = hb_scr[...]
    g = _dot(hb, wg_ref[...])
    u = _dot(hb, wu_ref[...])
    if seq_mode:
        tail = g[tm - sub:, :]
        prev = jnp.where(i % tiles_per_seq == 0, 0.0, carry_scr[f])
        carry_scr[f] = tail
        gout_ref[...] = tail
        row = lax.broadcasted_iota(jnp.int32, (sub, FF_TILE), 0)
        g1 = pltpu.roll(g, 1, 0)
        g2 = pltpu.roll(g, 2, 0)
        g1 = jnp.concatenate([jnp.where(row < 1, pltpu.roll(prev, 1, 0), g1[:sub]), g1[sub:]], axis=0)
        g2 = jnp.concatenate([jnp.where(row < 2, pltpu.roll(prev, 2, 0), g2[:sub]), g2[sub:]], axis=0)
    else:
        g2 = s0_ref[...]
        g1 = s1_ref[...]
        gout_ref[...] = g
    c = cb_ref[...] + cw_ref[0:1, :] * g2 + cw_ref[1:2, :] * g1 + cw_ref[2:3, :] * g
    act = (_gelu_tanh(c) * u).astype(_BF16)
    acc_scr[...] += _dot(act, wd_ref[...])

    @pl.when(f == nf - 1)
    def _():
        e = _dot(p_ref[...].astype(_BF16), wpe_ref[...])
        x1 = x_ref[...] + _rms(acc_scr[...], npost_ref[...])
        gate = 1.0 / (1.0 + jnp.exp(-_dot(x1.astype(_BF16), wpg_ref[...])))
        out_ref[...] = x1 + _rms(gate * e, nple_ref[...])


def _ffn(x, p, npre, npost, nple, wg, wu, cw, cb, wd, wpg, wpe, *, tm, tiles_per_seq, prev_rows=None):
    M, D = x.shape
    F = wg.shape[1]
    nf = F // FF_TILE
    nt = M // tm
    seq_mode = tiles_per_seq > 0
    vec = pl.BlockSpec((1, D), lambda i, f: (0, 0))
    in_specs = [
        pl.BlockSpec((tm, D), lambda i, f: (i, 0)),
        pl.BlockSpec((tm, p.shape[1]), lambda i, f: (i, 0)),
        vec, vec, vec,
        pl.BlockSpec((D, FF_TILE), lambda i, f: (0, f)),
        pl.BlockSpec((D, FF_TILE), lambda i, f: (0, f)),
        pl.BlockSpec((CONV_W, FF_TILE), lambda i, f: (0, f)),
        pl.BlockSpec((1, FF_TILE), lambda i, f: (0, f)),
        pl.BlockSpec((FF_TILE, D), lambda i, f: (f, 0)),
        pl.BlockSpec(wpg.shape, lambda i, f: (0, 0)),
        pl.BlockSpec(wpe.shape, lambda i, f: (0, 0)),
    ]
    args = [x, p, npre, npost, nple, wg, wu, cw, cb, wd, wpg, wpe]
    scratch = [pltpu.VMEM((tm, D), _BF16), pltpu.VMEM((tm, D), _F32)]
    if seq_mode:
        g_rows = V7X_SUBLANES
        scratch.append(pltpu.VMEM((nf, V7X_SUBLANES, FF_TILE), _F32))
    else:
        g_rows = tm
        in_specs += [pl.BlockSpec((tm, FF_TILE), lambda i, f: (i, f))] * 2
        args += list(prev_rows)
    return pl.pallas_call(
        functools.partial(_ffn_kernel, tm=tm, tiles_per_seq=tiles_per_seq, n_ff_tiles=nf),
        grid=(nt, nf),
        in_specs=in_specs,
        out_specs=[pl.BlockSpec((tm, D), lambda i, f: (i, 0)),
                   pl.BlockSpec((g_rows, FF_TILE), lambda i, f: (i, f))],
        out_shape=[jax.ShapeDtypeStruct((M, D), _F32),
                   jax.ShapeDtypeStruct((nt * g_rows, F), _F32)],
        scratch_shapes=scratch,
        compiler_params=_cparams(("arbitrary", "arbitrary"), 56 * 2**20),
        name="conv_ffn_ple",
    )(*args)


def _pool_band():
    t = jnp.arange(POOL_TILE)[:, None] + POOL_HALO
    c = jnp.arange(POOL_HALO + POOL_TILE)[None, :]
    return jnp.stack([((c <= t) & (c > t - w)).astype(_BF16) for w in POOL_WINDOWS])


def _later_keys_neg(n, copies):
    later = jnp.arange(n)[:, None] > jnp.arange(n)[None, :]
    return jnp.tile(-later.astype(_BF16), (copies, 1))


def kernel(x_prompt, x_sample, state_pool, state_conv, cache_k, cache_v, page_table, p_prompt, p_sample, norm_mix_pre, norm_mix_post, pool_w, pool_scale, sb_wqkv, sb_bias, sb_wo, norm_ffn_pre, norm_ffn_post, ffn_w_gate, ffn_w_up, ffn_conv_w, ffn_conv_b, ffn_w_down, ple_gate, ple_proj, ple_norm):
    Bp, S, D = x_prompt.shape
    Bs = x_sample.shape[0]
    depth = norm_mix_pre.shape[0]
    d_ff = ffn_w_gate.shape[2]
    past = page_table.shape[1] * cache_k.shape[2]
    Mp = Bp * S
    assert x_sample.shape[1] == 1 and S % ROW_TILE == 0 and S % POOL_TILE == 0 and S % ATT_BLOCK == 0
    assert d_ff % FF_TILE == 0 and D == D_MODEL

    xp = x_prompt
    xs = x_sample.reshape(Bs, D)
    vec = lambda a: a.reshape(1, -1)
    band = _pool_band()
    tri_p = _later_keys_neg(ATT_BLOCK, 1)
    tri_s = _later_keys_neg(cache_k.shape[2], 2)
    head_of_col = (jnp.arange(D) // HEAD_DIM)[None, :]
    cache_kt = jnp.transpose(cache_k, (0, 1, 3, 4, 2))
    cache_vt = jnp.transpose(cache_v, (0, 1, 3, 4, 2))

    pool_p, pool_s, conv_p, conv_s = [], [], [], []
    ks_l, vs_l = [], []
    kv_prompt = None
    for i in range(depth):
        j = i // 2
        npre, npost = vec(norm_mix_pre[i]), vec(norm_mix_post[i])
        if i % 2 == 0:
            wgrp = pool_w[j].astype(_BF16)
            scale = vec(pool_scale[j])
            xp, hlast = _pool_prompt(xp, band, wgrp, scale, npre, npost)
            pool_p.append(hlast[:, -POOL_BUF:, :])
            buf = state_pool[:, j]
            xs, hs = _pool_sample(xs, buf, wgrp, scale, npre, npost, past)
            pool_s.append(jnp.concatenate([buf[:, 1:], hs[:, None, :]], axis=1))
        else:
            wqkv = sb_wqkv[j].astype(_BF16)
            wo = sb_wo[j].astype(_BF16)
            q, kt, vt, kb, vb = _qkv(xp.reshape(Mp, D), npre, wqkv, ROW_TILE, leaves=(S, depth // 2, j, kv_prompt))
            kv_prompt = (kt, vt)
            o = _attn_prompt(q.reshape(Bp, S, D), kb.reshape(Bp, S, D), vb.reshape(Bp, S, D), tri_p, sb_bias[j])
            xp = _proj_res(xp.reshape(Mp, D), o.reshape(Mp, D), wo, npost, ROW_TILE).reshape(Bp, S, D)

            qs, ksn, vsn = _qkv(xs, npre, wqkv, Bs)
            q_bd = jnp.where(head_of_col[None] == jnp.arange(N_HEADS)[None, :, None], qs[:, None, :], 0)
            os_ = _attn_sample(q_bd, cache_kt, cache_vt, page_table, tri_s, sb_bias[j].reshape(N_HEADS, 1), j)
            xs = _proj_res(xs, os_.reshape(Bs, D).astype(_BF16), wo, npost, Bs)
            ks_l.append(ksn.reshape(Bs, 1, N_HEADS, HEAD_DIM))
            vs_l.append(vsn.reshape(Bs, 1, N_HEADS, HEAD_DIM))

        ffn_w = (vec(norm_ffn_pre[i]), vec(norm_ffn_post[i]), vec(ple_norm[i]),
                 ffn_w_gate[i].astype(_BF16), ffn_w_up[i].astype(_BF16), ffn_conv_w[i],
                 vec(ffn_conv_b[i]), ffn_w_down[i].astype(_BF16),
                 ple_gate[i].astype(_BF16), ple_proj[i].astype(_BF16))
        xp2, gtail = _ffn(xp.reshape(Mp, D), p_prompt[i].reshape(Mp, -1), *ffn_w,
                          tm=ROW_TILE, tiles_per_seq=S // ROW_TILE)
        xp = xp2.reshape(Bp, S, D)
        gtail = gtail.reshape(Bp, S // ROW_TILE, V7X_SUBLANES, d_ff)
        conv_p.append(gtail[:, -1, -(CONV_W - 1):, :])
        sc = state_conv[:, i]
        xs, gs = _ffn(xs, p_sample[i].reshape(Bs, -1), *ffn_w, tm=Bs, tiles_per_seq=0,
                      prev_rows=(sc[:, 0], sc[:, 1]))
        conv_s.append(jnp.stack([sc[:, 1], gs], axis=1))

    seq_minor = lambda a: jnp.transpose(a, (0, 1, 4, 2, 3))
    return (xp, xs.reshape(Bs, 1, D),
            jnp.stack(pool_p, axis=1), jnp.stack(pool_s, axis=1),
            jnp.stack(conv_p, axis=1), jnp.stack(conv_s, axis=1),
            seq_minor(kv_prompt[0]), seq_minor(kv_prompt[1]),
            jnp.stack(ks_l, axis=1), jnp.stack(vs_l, axis=1))
```

```python
import functools
import math

import jax
import jax.numpy as jnp
from jax import lax
from jax.experimental import pallas as pl
from jax.experimental.pallas import tpu as pltpu

D_MODEL = 1024
N_HEADS = 16
HEAD_DIM = D_MODEL // N_HEADS
POOL_WINDOWS = (2, 4, 8, 16)
POOL_GROUP = D_MODEL // len(POOL_WINDOWS)
POOL_BUF = max(POOL_WINDOWS) - 1
CONV_W = 3
EPS = 1e-6

V7X_LANES = 128
V7X_SUBLANES = 8
V7X_VMEM_BYTES = 64 * 1024 * 1024

ROW_TILE = 512
FF_TILE = 1408
POOL_TILE = 256
POOL_HALO = 128
ATT_BLOCK = 256
ATT_Q_ROWS = 2 * ATT_BLOCK
HEADS_PER_STEP = V7X_LANES // HEAD_DIM
PAGES_PER_STEP = 8

_F32 = jnp.float32
_BF16 = jnp.bfloat16


def _cparams(semantics, vmem_bytes):
    assert vmem_bytes <= V7X_VMEM_BYTES
    return pltpu.CompilerParams(dimension_semantics=semantics, vmem_limit_bytes=int(vmem_bytes))


def _rms(x, g):
    return x * lax.rsqrt(jnp.mean(x * x, axis=-1, keepdims=True) + EPS) * g


def _split_bf16(x):
    hi = x.astype(_BF16)
    lo = (x - hi.astype(_F32)).astype(_BF16)
    return hi, lo


def _dot(a, b):
    return jnp.dot(a, b, preferred_element_type=_F32)


def _softplus(z):
    return jnp.maximum(z, 0.0) + jnp.log(1.0 + jnp.exp(-jnp.abs(z)))


def _pool_prompt_kernel(x_ref, halo_ref, band_ref, wgrp_ref, scale_ref, npre_ref, npost_ref,
                        out_ref, hlast_ref, mix_scr):
    i = pl.program_id(1)
    xt = x_ref[0]
    h = _rms(xt, npre_ref[...])
    hh = _rms(halo_ref[0], npre_ref[...])
    hh = jnp.where(i == 0, 0.0, hh)
    hext = jnp.concatenate([hh, h], axis=0)
    hi, lo = _split_bf16(hext)
    pos = i * POOL_TILE + lax.broadcasted_iota(jnp.int32, (POOL_TILE, 1), 0)
    for g, w in enumerate(POOL_WINDOWS):
        cols = slice(g * POOL_GROUP, (g + 1) * POOL_GROUP)
        band = band_ref[g]
        wsum = _dot(band, hi[:, cols]) + _dot(band, lo[:, cols])
        cnt = jnp.minimum(pos + 1, w).astype(_F32)
        diff = wsum / cnt - h[:, cols]
        mix_scr[:, cols] = _dot(diff.astype(_BF16), wgrp_ref[g])
    mixed = mix_scr[...] * scale_ref[...]
    out_ref[0] = xt + _rms(mixed, npost_ref[...])
    hlast_ref[0] = h[POOL_TILE - 2 * V7X_SUBLANES:, :]


def _pool_prompt(x, band, wgrp, scale, npre, npost):
    B, S, D = x.shape
    nt = S // POOL_TILE
    halo_per_tile = POOL_TILE // POOL_HALO
    keep = 2 * V7X_SUBLANES
    vec = pl.BlockSpec((1, D), lambda b, i: (0, 0))
    return pl.pallas_call(
        _pool_prompt_kernel,
        grid=(B, nt),
        in_specs=[
            pl.BlockSpec((1, POOL_TILE, D), lambda b, i: (b, i, 0)),
            pl.BlockSpec((1, POOL_HALO, D), lambda b, i: (b, jnp.maximum(i * halo_per_tile - 1, 0), 0)),
            pl.BlockSpec(band.shape, lambda b, i: (0, 0, 0)),
            pl.BlockSpec(wgrp.shape, lambda b, i: (0, 0, 0)),
            vec, vec, vec,
        ],
        out_specs=[
            pl.BlockSpec((1, POOL_TILE, D), lambda b, i: (b, i, 0)),
            pl.BlockSpec((1, keep, D), lambda b, i: (b, i, 0)),
        ],
        out_shape=[jax.ShapeDtypeStruct((B, S, D), _F32),
                   jax.ShapeDtypeStruct((B, nt * keep, D), _F32)],
        scratch_shapes=[pltpu.VMEM((POOL_TILE, D), _F32)],
        compiler_params=_cparams(("arbitrary", "arbitrary"), 32 * 2**20),
        name="pool_prompt",
    )(x, x, band, wgrp, scale, npre, npost)


def _pool_sample_kernel(x_ref, buf_ref, wgrp_ref, scale_ref, npre_ref, npost_ref,
                        out_ref, h_ref, mix_scr, *, past):
    xt = x_ref[...]
    h = _rms(xt, npre_ref[...])
    row = lax.broadcasted_iota(jnp.int32, (1, POOL_BUF, 1), 1)
    for g, w in enumerate(POOL_WINDOWS):
        cols = slice(g * POOL_GROUP, (g + 1) * POOL_GROUP)
        in_window = (row >= POOL_BUF - (w - 1)).astype(_F32)
        wsum = jnp.sum(buf_ref[:, :, cols] * in_window, axis=1) + h[:, cols]
        diff = wsum / float(min(past + 1, w)) - h[:, cols]
        mix_scr[:, cols] = _dot(diff.astype(_BF16), wgrp_ref[g])
    mixed = mix_scr[...] * scale_ref[...]
    out_ref[...] = xt + _rms(mixed, npost_ref[...])
    h_ref[...] = h


def _pool_sample(x, buf, wgrp, scale, npre, npost, past):
    M, D = x.shape
    tb = 32
    vec = pl.BlockSpec((1, D), lambda i: (0, 0))
    return pl.pallas_call(
        functools.partial(_pool_sample_kernel, past=past),
        grid=(M // tb,),
        in_specs=[
            pl.BlockSpec((tb, D), lambda i: (i, 0)),
            pl.BlockSpec((tb, POOL_BUF, D), lambda i: (i, 0, 0)),
            pl.BlockSpec(wgrp.shape, lambda i: (0, 0, 0)),
            vec, vec, vec,
        ],
        out_specs=[pl.BlockSpec((tb, D), lambda i: (i, 0)), pl.BlockSpec((tb, D), lambda i: (i, 0))],
        out_shape=[jax.ShapeDtypeStruct((M, D), _F32), jax.ShapeDtypeStruct((M, D), _F32)],
        scratch_shapes=[pltpu.VMEM((tb, D), _F32)],
        compiler_params=_cparams(("arbitrary",), 32 * 2**20),
        name="pool_sample",
    )(x, buf, wgrp, scale, npre, npost)


def _qkv_kernel(*refs, seq_major_kv):
    x_ref, npre_ref, w_ref = refs[:3]
    q_ref, k_ref, v_ref = refs[-5:-2] if seq_major_kv else refs[-3:]
    hb = _rms(x_ref[...], npre_ref[...]).astype(_BF16)
    D = D_MODEL
    q = _dot(hb, w_ref[:, 0:D])
    q_ref[...] = (q * (HEAD_DIM ** -0.5)).astype(_BF16)
    for j, out_ref in enumerate((k_ref, v_ref)):
        kv = _dot(hb, w_ref[:, (j + 1) * D:(j + 2) * D])
        if seq_major_kv:
            refs[-2 + j][...] = kv.astype(_BF16)
            out_ref[...] = kv.T.reshape(N_HEADS, HEAD_DIM, kv.shape[0])
        else:
            out_ref[...] = kv


def _qkv(x, npre, w, tm, leaves=None):
    M, D = x.shape
    row = lambda dt: (pl.BlockSpec((tm, D), lambda i: (i, 0)), jax.ShapeDtypeStruct((M, D), dt))
    in_specs = [pl.BlockSpec((tm, D), lambda i: (i, 0)),
                pl.BlockSpec((1, D), lambda i: (0, 0)),
                pl.BlockSpec(w.shape, lambda i: (0, 0))]
    args = [x, npre, w]
    aliases = {}
    if leaves is None:
        outs = [row(_BF16), row(_F32), row(_F32)]
    else:
        seq_len, n_layers, layer, earlier = leaves
        tps = seq_len // tm
        leaf = (pl.BlockSpec((None, None, N_HEADS, HEAD_DIM, tm), lambda i: (i // tps, layer, 0, 0, i % tps)),
                jax.ShapeDtypeStruct((M // seq_len, n_layers, N_HEADS, HEAD_DIM, seq_len), _F32))
        outs = [row(_BF16), leaf, leaf, row(_BF16), row(_BF16)]
        if earlier is not None:
            in_specs += [pl.BlockSpec(memory_space=pl.ANY)] * 2
            args += list(earlier)
            aliases = {3: 1, 4: 2}
    return pl.pallas_call(
        functools.partial(_qkv_kernel, seq_major_kv=leaves is not None),
        grid=(M // tm,),
        in_specs=in_specs,
        out_specs=[o[0] for o in outs],
        out_shape=[o[1] for o in outs],
        input_output_aliases=aliases,
        compiler_params=_cparams(("arbitrary",), 48 * 2**20),
        name="rms_qkv",
    )(*args)


def _attn_block(qh, k_blk, v_blk, ntri_ref, bias, carry, key_minus_query):
    z = lax.dot_general(qh, k_blk, (((1,), (1,)), ((), ())), preferred_element_type=_F32) + bias
    sp = _softplus(z)
    log_beta = z - sp
    if key_minus_query is not None:
        shape = z.shape
        valid = (lax.broadcasted_iota(jnp.int32, shape, 1) + key_minus_query
                 < lax.broadcasted_iota(jnp.int32, shape, 0))
        sp = jnp.where(valid, sp, 0.0)
    a = jnp.exp(log_beta + _dot(sp.astype(_BF16), ntri_ref[...]) + carry)
    if key_minus_query is not None:
        a = jnp.where(valid, a, 0.0)
    return _dot(a.astype(_BF16), v_blk), jnp.sum(sp, axis=-1, keepdims=True)


def _attn_prompt_kernel(bias_ref, q_ref, k_ref, v_ref, ntri_ref, o_ref, qm_scr, acc_scr, carry_scr):
    hp = pl.program_id(1)
    qi = pl.program_id(2)
    n = ATT_BLOCK
    heads = range(HEADS_PER_STEP)
    q = q_ref[0]
    lane = lax.broadcasted_iota(jnp.int32, (1, V7X_LANES), 1)
    in_head = [(lane >= hh * HEAD_DIM) & (lane < (hh + 1) * HEAD_DIM) for hh in heads]
    for hh in heads:
        qm_scr[hh] = jnp.where(in_head[hh], q, jnp.zeros_like(q))
    bias = [bias_ref[hp * HEADS_PER_STEP + hh] for hh in heads]
    first = qi * (ATT_Q_ROWS // n)

    def kv(j):
        st = pl.multiple_of(j * n, n)
        return k_ref[0, pl.ds(st, n), :], v_ref[0, pl.ds(st, n), :]

    def two_blocks(kv0, kv1, offsets):
        for hh in heads:
            c0 = 0.0 if offsets else carry_scr[hh]
            pv0, rs0 = _attn_block(qm_scr[hh], *kv0, ntri_ref, bias[hh], c0, offsets[0] if offsets else None)
            c1 = c0 - rs0
            pv1, rs1 = _attn_block(qm_scr[hh], *kv1, ntri_ref, bias[hh], c1, offsets[1] if offsets else None)
            if offsets:
                acc_scr[hh] = pv0 + pv1
            else:
                acc_scr[hh] += pv0 + pv1
            carry_scr[hh] = c1 - rs1

    two_blocks(kv(first + 1), kv(first), (n, 0))

    def pair(t, _):
        j0 = first - 1 - 2 * t
        two_blocks(kv(j0), kv(j0 - 1), None)
        return 0

    lax.fori_loop(0, qi, pair, 0)

    out = acc_scr[0]
    for hh in heads[1:]:
        out = jnp.where(in_head[hh], acc_scr[hh], out)
    o_ref[0] = out.astype(o_ref.dtype)


def _attn_prompt(q, kb, vb, ntri, bias):
    B, S, D = q.shape
    n = ATT_Q_ROWS
    H = HEADS_PER_STEP
    return pl.pallas_call(
        _attn_prompt_kernel,
        grid_spec=pltpu.PrefetchScalarGridSpec(
            num_scalar_prefetch=1,
            grid=(B, D // V7X_LANES, S // n),
            in_specs=[
                pl.BlockSpec((1, n, V7X_LANES), lambda b, hp, qi, bias: (b, qi, hp)),
                pl.BlockSpec((1, S, V7X_LANES), lambda b, hp, qi, bias: (b, 0, hp)),
                pl.BlockSpec((1, S, V7X_LANES), lambda b, hp, qi, bias: (b, 0, hp)),
                pl.BlockSpec(ntri.shape, lambda b, hp, qi, bias: (0, 0)),
            ],
            out_specs=pl.BlockSpec((1, n, V7X_LANES), lambda b, hp, qi, bias: (b, qi, hp)),
            scratch_shapes=[pltpu.VMEM((H, n, V7X_LANES), _BF16), pltpu.VMEM((H, n, V7X_LANES), _F32),
                            pltpu.VMEM((H, n, 1), _F32)],
        ),
        out_shape=jax.ShapeDtypeStruct((B, S, D), _BF16),
        compiler_params=_cparams(("arbitrary", "arbitrary", "arbitrary"), 32 * 2**20),
        name="sb_attn_prompt",
    )(bias, q, kb, vb, ntri)


def _attn_sample_kernel(pt_ref, bias_ref, q_ref, ntri_ref, *refs):
    del pt_ref
    P = PAGES_PER_STEP
    k_refs, v_refs = refs[:P], refs[P:2 * P]
    o_ref, acc_scr, carry_scr = refs[2 * P:]
    step = pl.program_id(1)
    page = V7X_LANES

    @pl.when(step == 0)
    def _():
        acc_scr[...] = jnp.zeros_like(acc_scr)
        carry_scr[...] = jnp.zeros_like(carry_scr)

    def pages(page_refs):
        return jnp.concatenate([r[...].reshape(D_MODEL, page).astype(_BF16) for r in page_refs], axis=1)

    z = _dot(q_ref[0], pages(k_refs)) + bias_ref[...]
    sp = _softplus(z)
    hi, lo = _split_bf16(sp)
    carry = carry_scr[...]
    between = []
    for s in range(P):
        cols = slice(s * page, (s + 1) * page)
        between.append(_dot(jnp.concatenate([hi[:, cols], lo[:, cols]], axis=1), ntri_ref[...]) + carry)
        carry = carry - jnp.sum(sp[:, cols], axis=-1, keepdims=True)
    carry_scr[...] = carry
    a = jnp.exp(z - sp + jnp.concatenate(between, axis=1)).astype(_BF16)
    acc_scr[...] += lax.dot_general(a, pages(v_refs), (((1,), (1,)), ((), ())), preferred_element_type=_F32)

    @pl.when(step == pl.num_programs(1) - 1)
    def _():
        head = lax.broadcasted_iota(jnp.int32, (N_HEADS, D_MODEL), 0)
        col_head = lax.broadcasted_iota(jnp.int32, (N_HEADS, D_MODEL), 1) // HEAD_DIM
        o_ref[0] = jnp.sum(jnp.where(head == col_head, acc_scr[...], 0.0), axis=0, keepdims=True)


def _attn_sample(q_bd, cache_kt, cache_vt, page_table, ntri, bias_col, layer):
    M, rows, D = q_bd.shape
    n_pages = page_table.shape[1]
    page = cache_kt.shape[-1]
    P = PAGES_PER_STEP
    assert n_pages % P == 0 and page == V7X_LANES and rows == N_HEADS and D == D_MODEL

    def page_spec(s):
        def index(b, i, pt):
            return (pt[b, n_pages - 1 - (i * P + s)], layer, 0, 0, 0)
        return pl.BlockSpec((None, None, N_HEADS, HEAD_DIM, page), index)

    specs = [page_spec(s) for s in range(P)]
    return pl.pallas_call(
        _attn_sample_kernel,
        grid_spec=pltpu.PrefetchScalarGridSpec(
            num_scalar_prefetch=1,
            grid=(M, n_pages // P),
            in_specs=[
                pl.BlockSpec((rows, 1), lambda b, i, pt: (0, 0)),
                pl.BlockSpec((1, rows, D), lambda b, i, pt: (b, 0, 0)),
                pl.BlockSpec(ntri.shape, lambda b, i, pt: (0, 0)),
            ] + specs + specs,
            out_specs=pl.BlockSpec((1, 1, D), lambda b, i, pt: (b, 0, 0)),
            scratch_shapes=[pltpu.VMEM((rows, D), _F32), pltpu.VMEM((rows, 1), _F32)],
        ),
        out_shape=jax.ShapeDtypeStruct((M, 1, D), _F32),
        compiler_params=_cparams(("arbitrary", "arbitrary"), 56 * 2**20),
        name="sb_attn_sample",
    )(page_table, bias_col, q_bd, ntri, *([cache_kt] * P), *([cache_vt] * P))


def _proj_res_kernel(x_ref, o_ref, w_ref, npost_ref, out_ref):
    m = _dot(o_ref[...], w_ref[...])
    out_ref[...] = x_ref[...] + _rms(m, npost_ref[...])


def _proj_res(x, o, w, npost, tm):
    M, D = x.shape
    return pl.pallas_call(
        _proj_res_kernel,
        grid=(M // tm,),
        in_specs=[pl.BlockSpec((tm, D), lambda i: (i, 0)),
                  pl.BlockSpec((tm, D), lambda i: (i, 0)),
                  pl.BlockSpec(w.shape, lambda i: (0, 0)),
                  pl.BlockSpec((1, D), lambda i: (0, 0))],
        out_specs=pl.BlockSpec((tm, D), lambda i: (i, 0)),
        out_shape=jax.ShapeDtypeStruct((M, D), _F32),
        compiler_params=_cparams(("arbitrary",), 32 * 2**20),
        name="proj_res",
    )(x, o, w, npost)


def _gelu_tanh(c):
    return 0.5 * c * (1.0 + jnp.tanh(math.sqrt(2.0 / math.pi) * (c + 0.044715 * (c * c * c))))


def _ffn_kernel(*refs, tm, tiles_per_seq, n_ff_tiles):
    seq_mode = tiles_per_seq > 0
    nf = n_ff_tiles
    (x_ref, p_ref, npre_ref, npost_ref, nple_ref, wg_ref, wu_ref, cw_ref, cb_ref, wd_ref,
     wpg_ref, wpe_ref) = refs[:12]
    if seq_mode:
        out_ref, gout_ref, hb_scr, acc_scr, carry_scr = refs[12:]
    else:
        s0_ref, s1_ref, out_ref, gout_ref, hb_scr, acc_scr = refs[12:]
    i = pl.program_id(0)
    f = pl.program_id(1)
    sub = V7X_SUBLANES

    @pl.when(f == 0)
    def _():
        hb_scr[...] = _rms(x_ref[...], npre_ref[...]).astype(_BF16)
        acc_scr[...] = jnp.zeros_like(acc_scr)

    hb = hb_scr[...]
    g = _dot(hb, wg_ref[...])
    u = _dot(hb, wu_ref[...])
    if seq_mode:
        tail = g[tm - sub:, :]
        prev = jnp.where(i % tiles_per_seq == 0, 0.0, carry_scr[f])
        carry_scr[f] = tail
        gout_ref[...] = tail
        row = lax.broadcasted_iota(jnp.int32, (sub, FF_TILE), 0)
        g1 = pltpu.roll(g, 1, 0)
        g2 = pltpu.roll(g, 2, 0)
        g1 = jnp.concatenate([jnp.where(row < 1, pltpu.roll(prev, 1, 0), g1[:sub]), g1[sub:]], axis=0)
        g2 = jnp.concatenate([jnp.where(row < 2, pltpu.roll(prev, 2, 0), g2[:sub]), g2[sub:]], axis=0)
    else:
        g2 = s0_ref[...]
        g1 = s1_ref[...]
        gout_ref[...] = g
    c = cb_ref[...] + cw_ref[0:1, :] * g2 + cw_ref[1:2, :] * g1 + cw_ref[2:3, :] * g
    act = (_gelu_tanh(c) * u).astype(_BF16)
    acc_scr[...] += _dot(act, wd_ref[...])

    @pl.when(f == nf - 1)
    def _():
        e = _dot(p_ref[...].astype(_BF16), wpe_ref[...])
        x1 = x_ref[...] + _rms(acc_scr[...], npost_ref[...])
        gate = 1.0 / (1.0 + jnp.exp(-_dot(x1.astype(_BF16), wpg_ref[...])))
        out_ref[...] = x1 + _rms(gate * e, nple_ref[...])


def _ffn(x, p, layer, npre, npost, nple, wg, wu, cw, cb, wd, wpg, wpe, *, tm, tiles_per_seq, prev_rows=None):
    M, D = x.shape
    F = wg.shape[2]
    nf = F // FF_TILE
    nt = M // tm
    seq_mode = tiles_per_seq > 0
    vec = pl.BlockSpec((1, D), lambda i, f: (0, 0))
    in_specs = [
        pl.BlockSpec((tm, D), lambda i, f: (i, 0)),
        pl.BlockSpec((None, tm, p.shape[2]), lambda i, f: (layer, i, 0)),
        vec, vec, vec,
        pl.BlockSpec((None, D, FF_TILE), lambda i, f: (layer, 0, f)),
        pl.BlockSpec((None, D, FF_TILE), lambda i, f: (layer, 0, f)),
        pl.BlockSpec((None, CONV_W, FF_TILE), lambda i, f: (layer, 0, f)),
        pl.BlockSpec((None, 1, FF_TILE), lambda i, f: (layer, 0, f)),
        pl.BlockSpec((None, FF_TILE, D), lambda i, f: (layer, f, 0)),
        pl.BlockSpec((None,) + wpg.shape[1:], lambda i, f: (layer, 0, 0)),
        pl.BlockSpec((None,) + wpe.shape[1:], lambda i, f: (layer, 0, 0)),
    ]
    args = [x, p, npre, npost, nple, wg, wu, cw, cb, wd, wpg, wpe]
    scratch = [pltpu.VMEM((tm, D), _BF16), pltpu.VMEM((tm, D), _F32)]
    if seq_mode:
        g_rows = V7X_SUBLANES
        scratch.append(pltpu.VMEM((nf, V7X_SUBLANES, FF_TILE), _F32))
    else:
        g_rows = tm
        in_specs += [pl.BlockSpec((tm, FF_TILE), lambda i, f: (i, f))] * 2
        args += list(prev_rows)
    return pl.pallas_call(
        functools.partial(_ffn_kernel, tm=tm, tiles_per_seq=tiles_per_seq, n_ff_tiles=nf),
        grid=(nt, nf),
        in_specs=in_specs,
        out_specs=[pl.BlockSpec((tm, D), lambda i, f: (i, 0)),
                   pl.BlockSpec((g_rows, FF_TILE), lambda i, f: (i, f))],
        out_shape=[jax.ShapeDtypeStruct((M, D), _F32),
                   jax.ShapeDtypeStruct((nt * g_rows, F), _F32)],
        scratch_shapes=scratch,
        compiler_params=_cparams(("arbitrary", "arbitrary"), 56 * 2**20),
        name="conv_ffn_ple",
    )(*args)


def _pool_band():
    t = jnp.arange(POOL_TILE)[:, None] + POOL_HALO
    c = jnp.arange(POOL_HALO + POOL_TILE)[None, :]
    return jnp.stack([((c <= t) & (c > t - w)).astype(_BF16) for w in POOL_WINDOWS])


def _later_keys_neg(n, copies):
    later = jnp.arange(n)[:, None] > jnp.arange(n)[None, :]
    return jnp.tile(-later.astype(_BF16), (copies, 1))


def kernel(x_prompt, x_sample, state_pool, state_conv, cache_k, cache_v, page_table, p_prompt, p_sample, norm_mix_pre, norm_mix_post, pool_w, pool_scale, sb_wqkv, sb_bias, sb_wo, norm_ffn_pre, norm_ffn_post, ffn_w_gate, ffn_w_up, ffn_conv_w, ffn_conv_b, ffn_w_down, ple_gate, ple_proj, ple_norm):
    Bp, S, D = x_prompt.shape
    Bs = x_sample.shape[0]
    depth = norm_mix_pre.shape[0]
    d_ff = ffn_w_gate.shape[2]
    past = page_table.shape[1] * cache_k.shape[2]
    Mp = Bp * S
    assert x_sample.shape[1] == 1 and S % ROW_TILE == 0 and S % POOL_TILE == 0 and S % ATT_Q_ROWS == 0
    assert d_ff % FF_TILE == 0 and D == D_MODEL

    xp = x_prompt
    xs = x_sample.reshape(Bs, D)
    vec = lambda a: a.reshape(1, -1)
    band = _pool_band()
    tri_p = _later_keys_neg(ATT_BLOCK, 1)
    tri_s = _later_keys_neg(cache_k.shape[2], 2)
    head_of_col = (jnp.arange(D) // HEAD_DIM)[None, :]
    cache_kt = jnp.transpose(cache_k, (0, 1, 3, 4, 2))
    cache_vt = jnp.transpose(cache_v, (0, 1, 3, 4, 2))

    ffn_stacked = (ffn_w_gate.astype(_BF16), ffn_w_up.astype(_BF16), ffn_conv_w,
                   ffn_conv_b.reshape(depth, 1, d_ff), ffn_w_down.astype(_BF16),
                   ple_gate.astype(_BF16), ple_proj.astype(_BF16))

    pool_p, pool_s, conv_p, conv_s = [], [], [], []
    ks_l, vs_l = [], []
    kv_prompt = None
    for i in range(depth):
        j = i // 2
        npre, npost = vec(norm_mix_pre[i]), vec(norm_mix_post[i])
        if i % 2 == 0:
            wgrp = pool_w[j].astype(_BF16)
            scale = vec(pool_scale[j])
            xp, hlast = _pool_prompt(xp, band, wgrp, scale, npre, npost)
            pool_p.append(hlast[:, -POOL_BUF:, :])
            buf = state_pool[:, j]
            xs, hs = _pool_sample(xs, buf, wgrp, scale, npre, npost, past)
            pool_s.append(jnp.concatenate([buf[:, 1:], hs[:, None, :]], axis=1))
        else:
            wqkv = sb_wqkv[j].astype(_BF16)
            wo = sb_wo[j].astype(_BF16)
            q, kt, vt, kb, vb = _qkv(xp.reshape(Mp, D), npre, wqkv, ROW_TILE, leaves=(S, depth // 2, j, kv_prompt))
            kv_prompt = (kt, vt)
            o = _attn_prompt(q.reshape(Bp, S, D), kb.reshape(Bp, S, D), vb.reshape(Bp, S, D), tri_p, sb_bias[j])
            xp = _proj_res(xp.reshape(Mp, D), o.reshape(Mp, D), wo, npost, ROW_TILE).reshape(Bp, S, D)

            qs, ksn, vsn = _qkv(xs, npre, wqkv, Bs)
            q_bd = jnp.where(head_of_col[None] == jnp.arange(N_HEADS)[None, :, None], qs[:, None, :], 0)
            os_ = _attn_sample(q_bd, cache_kt, cache_vt, page_table, tri_s, sb_bias[j].reshape(N_HEADS, 1), j)
            xs = _proj_res(xs, os_.reshape(Bs, D).astype(_BF16), wo, npost, Bs)
            ks_l.append(ksn.reshape(Bs, 1, N_HEADS, HEAD_DIM))
            vs_l.append(vsn.reshape(Bs, 1, N_HEADS, HEAD_DIM))

        ffn_w = (i, vec(norm_ffn_pre[i]), vec(norm_ffn_post[i]), vec(ple_norm[i])) + ffn_stacked
        xp2, gtail = _ffn(xp.reshape(Mp, D), p_prompt.reshape(depth, Mp, -1), *ffn_w,
                          tm=ROW_TILE, tiles_per_seq=S // ROW_TILE)
        xp = xp2.reshape(Bp, S, D)
        gtail = gtail.reshape(Bp, S // ROW_TILE, V7X_SUBLANES, d_ff)
        conv_p.append(gtail[:, -1, -(CONV_W - 1):, :])
        sc = state_conv[:, i]
        xs, gs = _ffn(xs, p_sample.reshape(depth, Bs, -1), *ffn_w, tm=Bs, tiles_per_seq=0,
                      prev_rows=(sc[:, 0], sc[:, 1]))
        conv_s.append(jnp.stack([sc[:, 1], gs], axis=1))

    seq_minor = lambda a: jnp.transpose(a, (0, 1, 4, 2, 3))
    return (xp, xs.reshape(Bs, 1, D),
            jnp.stack(pool_p, axis=1), jnp.stack(pool_s, axis=1),
            jnp.stack(conv_p, axis=1), jnp.stack(conv_s, axis=1),
            seq_minor(kv_prompt[0]), seq_minor(kv_prompt[1]),
            jnp.stack(ks_l, axis=1), jnp.stack(vs_l, axis=1))
```

```python
import functools
import math

import jax
import jax.numpy as jnp
from jax import lax
from jax.experimental import pallas as pl
from jax.experimental.pallas import tpu as pltpu

D_MODEL = 1024
N_HEADS = 16
HEAD_DIM = D_MODEL // N_HEADS
POOL_WINDOWS = (2, 4, 8, 16)
POOL_GROUP = D_MODEL // len(POOL_WINDOWS)
POOL_BUF = max(POOL_WINDOWS) - 1
CONV_W = 3
EPS = 1e-6

V7X_LANES = 128
V7X_SUBLANES = 8
V7X_VMEM_BYTES = 64 * 1024 * 1024

ROW_TILE = 512
FF_TILE = 1408
POOL_TILE = 256
POOL_SUBTILES = 2
POOL_HALO = 128
ATT_BLOCK = 256
ATT_Q_ROWS = 2 * ATT_BLOCK
HEADS_PER_STEP = V7X_LANES // HEAD_DIM
PAGES_PER_STEP = 8

_F32 = jnp.float32
_BF16 = jnp.bfloat16


def _cparams(semantics, vmem_bytes):
    assert vmem_bytes <= V7X_VMEM_BYTES
    return pltpu.CompilerParams(dimension_semantics=semantics, vmem_limit_bytes=int(vmem_bytes))


def _rms(x, g):
    return x * lax.rsqrt(jnp.mean(x * x, axis=-1, keepdims=True) + EPS) * g


def _split_bf16(x):
    hi = x.astype(_BF16)
    lo = (x - hi.astype(_F32)).astype(_BF16)
    return hi, lo


def _dot(a, b):
    return jnp.dot(a, b, preferred_element_type=_F32)


def _softplus(z):
    return jnp.maximum(z, 0.0) + jnp.log(1.0 + jnp.exp(-jnp.abs(z)))


def _pool_prompt_kernel(x_ref, halo_ref, band_ref, wgrp_ref, scale_ref, npre_ref, npost_ref,
                        out_ref, hlast_ref, mix_scr):
    i = pl.program_id(1)
    halo = _rms(halo_ref[0], npre_ref[...])
    halo = jnp.where(i == 0, 0.0, halo)
    for s in range(POOL_SUBTILES):
        rows = slice(s * POOL_TILE, (s + 1) * POOL_TILE)
        xt = x_ref[0, rows, :]
        h = _rms(xt, npre_ref[...])
        hext = jnp.concatenate([halo, h], axis=0)
        hi, lo = _split_bf16(hext)
        pos = (i * POOL_SUBTILES + s) * POOL_TILE + lax.broadcasted_iota(jnp.int32, (POOL_TILE, 1), 0)
        for g, w in enumerate(POOL_WINDOWS):
            cols = slice(g * POOL_GROUP, (g + 1) * POOL_GROUP)
            band = band_ref[g]
            wsum = _dot(band, hi[:, cols]) + _dot(band, lo[:, cols])
            cnt = jnp.minimum(pos + 1, w).astype(_F32)
            diff = wsum / cnt - h[:, cols]
            mix_scr[rows, cols] = _dot(diff.astype(_BF16), wgrp_ref[g])
        mixed = mix_scr[rows, :] * scale_ref[...]
        out_ref[0, rows, :] = xt + _rms(mixed, npost_ref[...])
        halo = h[POOL_TILE - POOL_HALO:, :]
    hlast_ref[0] = h[POOL_TILE - 2 * V7X_SUBLANES:, :]


def _pool_prompt(x, band, wgrp, scale, npre, npost):
    B, S, D = x.shape
    step = POOL_SUBTILES * POOL_TILE
    nt = S // step
    halo_per_step = step // POOL_HALO
    keep = 2 * V7X_SUBLANES
    vec = pl.BlockSpec((1, D), lambda b, i: (0, 0))
    return pl.pallas_call(
        _pool_prompt_kernel,
        grid=(B, nt),
        in_specs=[
            pl.BlockSpec((1, step, D), lambda b, i: (b, i, 0)),
            pl.BlockSpec((1, POOL_HALO, D), lambda b, i: (b, jnp.maximum(i * halo_per_step - 1, 0), 0)),
            pl.BlockSpec(band.shape, lambda b, i: (0, 0, 0)),
            pl.BlockSpec(wgrp.shape, lambda b, i: (0, 0, 0)),
            vec, vec, vec,
        ],
        out_specs=[
            pl.BlockSpec((1, step, D), lambda b, i: (b, i, 0)),
            pl.BlockSpec((1, keep, D), lambda b, i: (b, i, 0)),
        ],
        out_shape=[jax.ShapeDtypeStruct((B, S, D), _F32),
                   jax.ShapeDtypeStruct((B, nt * keep, D), _F32)],
        scratch_shapes=[pltpu.VMEM((step, D), _F32)],
        compiler_params=_cparams(("arbitrary", "arbitrary"), 32 * 2**20),
        name="pool_prompt",
    )(x, x, band, wgrp, scale, npre, npost)


def _pool_sample_kernel(x_ref, buf_ref, wgrp_ref, scale_ref, npre_ref, npost_ref,
                        out_ref, h_ref, mix_scr, *, past):
    xt = x_ref[...]
    h = _rms(xt, npre_ref[...])
    row = lax.broadcasted_iota(jnp.int32, (1, POOL_BUF, 1), 1)
    for g, w in enumerate(POOL_WINDOWS):
        cols = slice(g * POOL_GROUP, (g + 1) * POOL_GROUP)
        in_window = (row >= POOL_BUF - (w - 1)).astype(_F32)
        wsum = jnp.sum(buf_ref[:, :, cols] * in_window, axis=1) + h[:, cols]
        diff = wsum / float(min(past + 1, w)) - h[:, cols]
        mix_scr[:, cols] = _dot(diff.astype(_BF16), wgrp_ref[g])
    mixed = mix_scr[...] * scale_ref[...]
    out_ref[...] = xt + _rms(mixed, npost_ref[...])
    h_ref[...] = h


def _pool_sample(x, buf, wgrp, scale, npre, npost, past):
    M, D = x.shape
    tb = 32
    vec = pl.BlockSpec((1, D), lambda i: (0, 0))
    return pl.pallas_call(
        functools.partial(_pool_sample_kernel, past=past),
        grid=(M // tb,),
        in_specs=[
            pl.BlockSpec((tb, D), lambda i: (i, 0)),
            pl.BlockSpec((tb, POOL_BUF, D), lambda i: (i, 0, 0)),
            pl.BlockSpec(wgrp.shape, lambda i: (0, 0, 0)),
            vec, vec, vec,
        ],
        out_specs=[pl.BlockSpec((tb, D), lambda i: (i, 0)), pl.BlockSpec((tb, D), lambda i: (i, 0))],
        out_shape=[jax.ShapeDtypeStruct((M, D), _F32), jax.ShapeDtypeStruct((M, D), _F32)],
        scratch_shapes=[pltpu.VMEM((tb, D), _F32)],
        compiler_params=_cparams(("arbitrary",), 32 * 2**20),
        name="pool_sample",
    )(x, buf, wgrp, scale, npre, npost)


def _qkv_kernel(*refs, seq_major_kv, first_layer_of):
    x_ref, npre_ref, w_ref = refs[:3]
    q_ref, k_ref, v_ref = refs[-5:-2] if seq_major_kv else refs[-3:]
    hb = _rms(x_ref[...], npre_ref[...]).astype(_BF16)
    D = D_MODEL
    q = _dot(hb, w_ref[:, 0:D])
    q_ref[...] = (q * (HEAD_DIM ** -0.5)).astype(_BF16)
    for j, out_ref in enumerate((k_ref, v_ref)):
        kv = _dot(hb, w_ref[:, (j + 1) * D:(j + 2) * D])
        if not seq_major_kv:
            out_ref[...] = kv
            continue
        refs[-2 + j][...] = kv.astype(_BF16)
        kv_t = kv.T.reshape(N_HEADS, HEAD_DIM, kv.shape[0])
        if first_layer_of is None:
            out_ref[...] = kv_t
        else:
            layer, n_layers = first_layer_of
            for l in range(n_layers):
                out_ref[l] = kv_t if l == layer else jnp.zeros_like(kv_t)


def _qkv(x, npre, w, tm, leaves=None):
    M, D = x.shape
    row = lambda dt: (pl.BlockSpec((tm, D), lambda i: (i, 0)), jax.ShapeDtypeStruct((M, D), dt))
    in_specs = [pl.BlockSpec((tm, D), lambda i: (i, 0)),
                pl.BlockSpec((1, D), lambda i: (0, 0)),
                pl.BlockSpec(w.shape, lambda i: (0, 0))]
    args = [x, npre, w]
    aliases = {}
    first_layer_of = None
    if leaves is None:
        outs = [row(_BF16), row(_F32), row(_F32)]
    else:
        seq_len, n_layers, layer, earlier = leaves
        tps = seq_len // tm
        leaf_shape = jax.ShapeDtypeStruct((M // seq_len, n_layers, N_HEADS, HEAD_DIM, seq_len), _F32)
        if earlier is None:
            first_layer_of = (layer, n_layers)
            leaf_spec = pl.BlockSpec((None, n_layers, N_HEADS, HEAD_DIM, tm), lambda i: (i // tps, 0, 0, 0, i % tps))
        else:
            leaf_spec = pl.BlockSpec((None, None, N_HEADS, HEAD_DIM, tm), lambda i: (i // tps, layer, 0, 0, i % tps))
            in_specs += [pl.BlockSpec(memory_space=pl.ANY)] * 2
            args += list(earlier)
            aliases = {3: 1, 4: 2}
        outs = [row(_BF16), (leaf_spec, leaf_shape), (leaf_spec, leaf_shape), row(_BF16), row(_BF16)]
    return pl.pallas_call(
        functools.partial(_qkv_kernel, seq_major_kv=leaves is not None, first_layer_of=first_layer_of),
        grid=(M // tm,),
        in_specs=in_specs,
        out_specs=[o[0] for o in outs],
        out_shape=[o[1] for o in outs],
        input_output_aliases=aliases,
        compiler_params=_cparams(("arbitrary",), 56 * 2**20),
        name="rms_qkv",
    )(*args)


def _attn_block(qh, k_blk, v_blk, ntri_ref, bias, carry, key_minus_query):
    z = lax.dot_general(qh, k_blk, (((1,), (1,)), ((), ())), preferred_element_type=_F32) + bias
    sp = _softplus(z)
    log_beta = z - sp
    if key_minus_query is not None:
        shape = z.shape
        valid = (lax.broadcasted_iota(jnp.int32, shape, 1) + key_minus_query
                 < lax.broadcasted_iota(jnp.int32, shape, 0))
        sp = jnp.where(valid, sp, 0.0)
    a = jnp.exp(log_beta + _dot(sp.astype(_BF16), ntri_ref[...]) + carry)
    if key_minus_query is not None:
        a = jnp.where(valid, a, 0.0)
    return _dot(a.astype(_BF16), v_blk), jnp.sum(sp, axis=-1, keepdims=True)


def _attn_prompt_kernel(bias_ref, q_ref, k_ref, v_ref, ntri_ref, o_ref, qm_scr, acc_scr, carry_scr):
    hp = pl.program_id(1)
    qi = pl.program_id(2)
    n = ATT_BLOCK
    heads = range(HEADS_PER_STEP)
    q = q_ref[0]
    lane = lax.broadcasted_iota(jnp.int32, (1, V7X_LANES), 1)
    in_head = [(lane >= hh * HEAD_DIM) & (lane < (hh + 1) * HEAD_DIM) for hh in heads]
    for hh in heads:
        qm_scr[hh] = jnp.where(in_head[hh], q, jnp.zeros_like(q))
    bias = [bias_ref[hp * HEADS_PER_STEP + hh] for hh in heads]
    first = qi * (ATT_Q_ROWS // n)

    def kv(j):
        st = pl.multiple_of(j * n, n)
        return k_ref[0, pl.ds(st, n), :], v_ref[0, pl.ds(st, n), :]

    def two_blocks(kv0, kv1, offsets):
        for hh in heads:
            c0 = 0.0 if offsets else carry_scr[hh]
            pv0, rs0 = _attn_block(qm_scr[hh], *kv0, ntri_ref, bias[hh], c0, offsets[0] if offsets else None)
            c1 = c0 - rs0
            pv1, rs1 = _attn_block(qm_scr[hh], *kv1, ntri_ref, bias[hh], c1, offsets[1] if offsets else None)
            if offsets:
                acc_scr[hh] = pv0 + pv1
            else:
                acc_scr[hh] += pv0 + pv1
            carry_scr[hh] = c1 - rs1

    two_blocks(kv(first + 1), kv(first), (n, 0))

    def kv_static(j):
        return k_ref[0, j * n:(j + 1) * n, :], v_ref[0, j * n:(j + 1) * n, :]

    for steps_before in range(1, k_ref.shape[1] // ATT_Q_ROWS):
        @pl.when(qi == steps_before)
        def _(steps_before=steps_before):
            for j0 in range(2 * steps_before - 1, 0, -2):
                two_blocks(kv_static(j0), kv_static(j0 - 1), None)

    out = acc_scr[0]
    for hh in heads[1:]:
        out = jnp.where(in_head[hh], acc_scr[hh], out)
    o_ref[0] = out.astype(o_ref.dtype)


def _attn_prompt(q, kb, vb, ntri, bias):
    B, S, D = q.shape
    n = ATT_Q_ROWS
    H = HEADS_PER_STEP
    return pl.pallas_call(
        _attn_prompt_kernel,
        grid_spec=pltpu.PrefetchScalarGridSpec(
            num_scalar_prefetch=1,
            grid=(B, D // V7X_LANES, S // n),
            in_specs=[
                pl.BlockSpec((1, n, V7X_LANES), lambda b, hp, qi, bias: (b, qi, hp)),
                pl.BlockSpec((1, S, V7X_LANES), lambda b, hp, qi, bias: (b, 0, hp)),
                pl.BlockSpec((1, S, V7X_LANES), lambda b, hp, qi, bias: (b, 0, hp)),
                pl.BlockSpec(ntri.shape, lambda b, hp, qi, bias: (0, 0)),
            ],
            out_specs=pl.BlockSpec((1, n, V7X_LANES), lambda b, hp, qi, bias: (b, qi, hp)),
            scratch_shapes=[pltpu.VMEM((H, n, V7X_LANES), _BF16), pltpu.VMEM((H, n, V7X_LANES), _F32),
                            pltpu.VMEM((H, n, 1), _F32)],
        ),
        out_shape=jax.ShapeDtypeStruct((B, S, D), _BF16),
        compiler_params=_cparams(("arbitrary", "arbitrary", "arbitrary"), 32 * 2**20),
        name="sb_attn_prompt",
    )(bias, q, kb, vb, ntri)


def _attn_sample_kernel(pt_ref, bias_ref, q_ref, ntri_ref, *refs):
    del pt_ref
    P = PAGES_PER_STEP
    k_refs, v_refs = refs[:P], refs[P:2 * P]
    o_ref, acc_scr, carry_scr = refs[2 * P:]
    step = pl.program_id(1)
    page = V7X_LANES

    @pl.when(step == 0)
    def _():
        acc_scr[...] = jnp.zeros_like(acc_scr)
        carry_scr[...] = jnp.zeros_like(carry_scr)

    def pages(page_refs):
        return jnp.concatenate([r[...].reshape(D_MODEL, page).astype(_BF16) for r in page_refs], axis=1)

    z = _dot(q_ref[0], pages(k_refs)) + bias_ref[...]
    sp = _softplus(z)
    hi, lo = _split_bf16(sp)
    carry = carry_scr[...]
    between = []
    for s in range(P):
        cols = slice(s * page, (s + 1) * page)
        between.append(_dot(jnp.concatenate([hi[:, cols], lo[:, cols]], axis=1), ntri_ref[...]) + carry)
        carry = carry - jnp.sum(sp[:, cols], axis=-1, keepdims=True)
    carry_scr[...] = carry
    a = jnp.exp(z - sp + jnp.concatenate(between, axis=1)).astype(_BF16)
    acc_scr[...] += lax.dot_general(a, pages(v_refs), (((1,), (1,)), ((), ())), preferred_element_type=_F32)

    @pl.when(step == pl.num_programs(1) - 1)
    def _():
        head = lax.broadcasted_iota(jnp.int32, (N_HEADS, D_MODEL), 0)
        col_head = lax.broadcasted_iota(jnp.int32, (N_HEADS, D_MODEL), 1) // HEAD_DIM
        o_ref[0] = jnp.sum(jnp.where(head == col_head, acc_scr[...], 0.0), axis=0, keepdims=True)


def _attn_sample(q_bd, cache_kt, cache_vt, page_table, ntri, bias_col, layer):
    M, rows, D = q_bd.shape
    n_pages = page_table.shape[1]
    page = cache_kt.shape[-1]
    P = PAGES_PER_STEP
    assert n_pages % P == 0 and page == V7X_LANES and rows == N_HEADS and D == D_MODEL

    def page_spec(s):
        def index(b, i, pt):
            return (pt[b, n_pages - 1 - (i * P + s)], layer, 0, 0, 0)
        return pl.BlockSpec((None, None, N_HEADS, HEAD_DIM, page), index)

    specs = [page_spec(s) for s in range(P)]
    return pl.pallas_call(
        _attn_sample_kernel,
        grid_spec=pltpu.PrefetchScalarGridSpec(
            num_scalar_prefetch=1,
            grid=(M, n_pages // P),
            in_specs=[
                pl.BlockSpec((rows, 1), lambda b, i, pt: (0, 0)),
                pl.BlockSpec((1, rows, D), lambda b, i, pt: (b, 0, 0)),
                pl.BlockSpec(ntri.shape, lambda b, i, pt: (0, 0)),
            ] + specs + specs,
            out_specs=pl.BlockSpec((1, 1, D), lambda b, i, pt: (b, 0, 0)),
            scratch_shapes=[pltpu.VMEM((rows, D), _F32), pltpu.VMEM((rows, 1), _F32)],
        ),
        out_shape=jax.ShapeDtypeStruct((M, 1, D), _F32),
        compiler_params=_cparams(("arbitrary", "arbitrary"), 56 * 2**20),
        name="sb_attn_sample",
    )(page_table, bias_col, q_bd, ntri, *([cache_kt] * P), *([cache_vt] * P))


def _proj_res_kernel(x_ref, o_ref, w_ref, npost_ref, out_ref):
    m = _dot(o_ref[...], w_ref[...])
    out_ref[...] = x_ref[...] + _rms(m, npost_ref[...])


def _proj_res(x, o, w, npost, tm):
    M, D = x.shape
    return pl.pallas_call(
        _proj_res_kernel,
        grid=(M // tm,),
        in_specs=[pl.BlockSpec((tm, D), lambda i: (i, 0)),
                  pl.BlockSpec((tm, D), lambda i: (i, 0)),
                  pl.BlockSpec(w.shape, lambda i: (0, 0)),
                  pl.BlockSpec((1, D), lambda i: (0, 0))],
        out_specs=pl.BlockSpec((tm, D), lambda i: (i, 0)),
        out_shape=jax.ShapeDtypeStruct((M, D), _F32),
        compiler_params=_cparams(("arbitrary",), 32 * 2**20),
        name="proj_res",
    )(x, o, w, npost)


def _gelu_tanh(c):
    return 0.5 * c * (1.0 + jnp.tanh(math.sqrt(2.0 / math.pi) * (c + 0.044715 * (c * c * c))))


def _ffn_kernel(*refs, tm, tiles_per_seq, n_ff_tiles):
    seq_mode = tiles_per_seq > 0
    nf = n_ff_tiles
    (x_ref, p_ref, npre_ref, npost_ref, nple_ref, wg_ref, wu_ref, cw_ref, cb_ref, wd_ref,
     wpg_ref, wpe_ref) = refs[:12]
    if seq_mode:
        out_ref, gout_ref, hb_scr, acc_scr, carry_scr = refs[12:]
    else:
        s0_ref, s1_ref, out_ref, gout_ref, hb_scr, acc_scr = refs[12:]
    i = pl.program_id(0)
    f = pl.program_id(1)
    sub = V7X_SUBLANES

    @pl.when(f == 0)
    def _():
        hb_scr[...] = _rms(x_ref[...], npre_ref[...]).astype(_BF16)
        acc_scr[...] = jnp.zeros_like(acc_scr)

    hb = hb_scr[...]
    g = _dot(hb, wg_ref[...])
    u = _dot(hb, wu_ref[...])
    if seq_mode:
        tail = g[tm - sub:, :]
        prev = jnp.where(i % tiles_per_seq == 0, 0.0, carry_scr[f])
        carry_scr[f] = tail
        gout_ref[...] = tail
        row = lax.broadcasted_iota(jnp.int32, (sub, FF_TILE), 0)
        g1 = pltpu.roll(g, 1, 0)
        g2 = pltpu.roll(g, 2, 0)
        g1 = jnp.concatenate([jnp.where(row < 1, pltpu.roll(prev, 1, 0), g1[:sub]), g1[sub:]], axis=0)
        g2 = jnp.concatenate([jnp.where(row < 2, pltpu.roll(prev, 2, 0), g2[:sub]), g2[sub:]], axis=0)
    else:
        g2 = s0_ref[...]
        g1 = s1_ref[...]
        gout_ref[...] = g
    c = cb_ref[...] + cw_ref[0:1, :] * g2 + cw_ref[1:2, :] * g1 + cw_ref[2:3, :] * g
    act = (_gelu_tanh(c) * u).astype(_BF16)
    acc_scr[...] += _dot(act, wd_ref[...])

    @pl.when(f == nf - 1)
    def _():
        e = _dot(p_ref[...].astype(_BF16), wpe_ref[...])
        x1 = x_ref[...] + _rms(acc_scr[...], npost_ref[...])
        gate = 1.0 / (1.0 + jnp.exp(-_dot(x1.astype(_BF16), wpg_ref[...])))
        out_ref[...] = x1 + _rms(gate * e, nple_ref[...])


def _ffn(x, p, layer, npre, npost, nple, wg, wu, cw, cb, wd, wpg, wpe, *, tm, tiles_per_seq, prev_rows=None):
    M, D = x.shape
    F = wg.shape[2]
    nf = F // FF_TILE
    nt = M // tm
    seq_mode = tiles_per_seq > 0
    vec = pl.BlockSpec((1, D), lambda i, f: (0, 0))
    in_specs = [
        pl.BlockSpec((tm, D), lambda i, f: (i, 0)),
        pl.BlockSpec((None, tm, p.shape[2]), lambda i, f: (layer, i, 0)),
        vec, vec, vec,
        pl.BlockSpec((None, D, FF_TILE), lambda i, f: (layer, 0, f)),
        pl.BlockSpec((None, D, FF_TILE), lambda i, f: (layer, 0, f)),
        pl.BlockSpec((None, CONV_W, FF_TILE), lambda i, f: (layer, 0, f)),
        pl.BlockSpec((None, 1, FF_TILE), lambda i, f: (layer, 0, f)),
        pl.BlockSpec((None, FF_TILE, D), lambda i, f: (layer, f, 0)),
        pl.BlockSpec((None,) + wpg.shape[1:], lambda i, f: (layer, 0, 0)),
        pl.BlockSpec((None,) + wpe.shape[1:], lambda i, f: (layer, 0, 0)),
    ]
    args = [x, p, npre, npost, nple, wg, wu, cw, cb, wd, wpg, wpe]
    scratch = [pltpu.VMEM((tm, D), _BF16), pltpu.VMEM((tm, D), _F32)]
    if seq_mode:
        g_rows = V7X_SUBLANES
        scratch.append(pltpu.VMEM((nf, V7X_SUBLANES, FF_TILE), _F32))
    else:
        g_rows = tm
        in_specs += [pl.BlockSpec((tm, FF_TILE), lambda i, f: (i, f))] * 2
        args += list(prev_rows)
    return pl.pallas_call(
        functools.partial(_ffn_kernel, tm=tm, tiles_per_seq=tiles_per_seq, n_ff_tiles=nf),
        grid=(nt, nf),
        in_specs=in_specs,
        out_specs=[pl.BlockSpec((tm, D), lambda i, f: (i, 0)),
                   pl.BlockSpec((g_rows, FF_TILE), lambda i, f: (i, f))],
        out_shape=[jax.ShapeDtypeStruct((M, D), _F32),
                   jax.ShapeDtypeStruct((nt * g_rows, F), _F32)],
        scratch_shapes=scratch,
        compiler_params=_cparams(("arbitrary", "arbitrary"), 56 * 2**20),
        name="conv_ffn_ple",
    )(*args)


def _pool_band():
    t = jnp.arange(POOL_TILE)[:, None] + POOL_HALO
    c = jnp.arange(POOL_HALO + POOL_TILE)[None, :]
    return jnp.stack([((c <= t) & (c > t - w)).astype(_BF16) for w in POOL_WINDOWS])


def _later_keys_neg(n, copies):
    later = jnp.arange(n)[:, None] > jnp.arange(n)[None, :]
    return jnp.tile(-later.astype(_BF16), (copies, 1))


def kernel(x_prompt, x_sample, state_pool, state_conv, cache_k, cache_v, page_table, p_prompt, p_sample, norm_mix_pre, norm_mix_post, pool_w, pool_scale, sb_wqkv, sb_bias, sb_wo, norm_ffn_pre, norm_ffn_post, ffn_w_gate, ffn_w_up, ffn_conv_w, ffn_conv_b, ffn_w_down, ple_gate, ple_proj, ple_norm):
    Bp, S, D = x_prompt.shape
    Bs = x_sample.shape[0]
    depth = norm_mix_pre.shape[0]
    d_ff = ffn_w_gate.shape[2]
    past = page_table.shape[1] * cache_k.shape[2]
    Mp = Bp * S
    assert x_sample.shape[1] == 1 and S % ROW_TILE == 0 and S % (POOL_SUBTILES * POOL_TILE) == 0 and S % ATT_Q_ROWS == 0
    assert d_ff % FF_TILE == 0 and D == D_MODEL

    xp = x_prompt
    xs = x_sample.reshape(Bs, D)
    vec = lambda a: a.reshape(1, -1)
    band = _pool_band()
    tri_p = _later_keys_neg(ATT_BLOCK, 1)
    tri_s = _later_keys_neg(cache_k.shape[2], 2)
    head_of_col = (jnp.arange(D) // HEAD_DIM)[None, :]
    cache_kt = jnp.transpose(cache_k, (0, 1, 3, 4, 2))
    cache_vt = jnp.transpose(cache_v, (0, 1, 3, 4, 2))

    ffn_stacked = (ffn_w_gate.astype(_BF16), ffn_w_up.astype(_BF16), ffn_conv_w,
                   ffn_conv_b.reshape(depth, 1, d_ff), ffn_w_down.astype(_BF16),
                   ple_gate.astype(_BF16), ple_proj.astype(_BF16))

    pool_p, pool_s, conv_p, conv_s = [], [], [], []
    ks_l, vs_l = [], []
    kv_prompt = None
    for i in range(depth):
        j = i // 2
        npre, npost = vec(norm_mix_pre[i]), vec(norm_mix_post[i])
        if i % 2 == 0:
            wgrp = pool_w[j].astype(_BF16)
            scale = vec(pool_scale[j])
            xp, hlast = _pool_prompt(xp, band, wgrp, scale, npre, npost)
            pool_p.append(hlast[:, -POOL_BUF:, :])
            buf = state_pool[:, j]
            xs, hs = _pool_sample(xs, buf, wgrp, scale, npre, npost, past)
            pool_s.append(jnp.concatenate([buf[:, 1:], hs[:, None, :]], axis=1))
        else:
            wqkv = sb_wqkv[j].astype(_BF16)
            wo = sb_wo[j].astype(_BF16)
            q, kt, vt, kb, vb = _qkv(xp.reshape(Mp, D), npre, wqkv, ROW_TILE, leaves=(S, depth // 2, j, kv_prompt))
            kv_prompt = (kt, vt)
            o = _attn_prompt(q.reshape(Bp, S, D), kb.reshape(Bp, S, D), vb.reshape(Bp, S, D), tri_p, sb_bias[j])
            xp = _proj_res(xp.reshape(Mp, D), o.reshape(Mp, D), wo, npost, ROW_TILE).reshape(Bp, S, D)

            qs, ksn, vsn = _qkv(xs, npre, wqkv, Bs)
            q_bd = jnp.where(head_of_col[None] == jnp.arange(N_HEADS)[None, :, None], qs[:, None, :], 0)
            os_ = _attn_sample(q_bd, cache_kt, cache_vt, page_table, tri_s, sb_bias[j].reshape(N_HEADS, 1), j)
            xs = _proj_res(xs, os_.reshape(Bs, D).astype(_BF16), wo, npost, Bs)
            ks_l.append(ksn.reshape(Bs, 1, N_HEADS, HEAD_DIM))
            vs_l.append(vsn.reshape(Bs, 1, N_HEADS, HEAD_DIM))

        ffn_w = (i, vec(norm_ffn_pre[i]), vec(norm_ffn_post[i]), vec(ple_norm[i])) + ffn_stacked
        xp2, gtail = _ffn(xp.reshape(Mp, D), p_prompt.reshape(depth, Mp, -1), *ffn_w,
                          tm=ROW_TILE, tiles_per_seq=S // ROW_TILE)
        xp = xp2.reshape(Bp, S, D)
        gtail = gtail.reshape(Bp, S // ROW_TILE, V7X_SUBLANES, d_ff)
        conv_p.append(gtail[:, -1, -(CONV_W - 1):, :])
        sc = state_conv[:, i]
        xs, gs = _ffn(xs, p_sample.reshape(depth, Bs, -1), *ffn_w, tm=Bs, tiles_per_seq=0,
                      prev_rows=(sc[:, 0], sc[:, 1]))
        conv_s.append(jnp.stack([sc[:, 1], gs], axis=1))

    seq_minor = lambda a: jnp.transpose(a, (0, 1, 4, 2, 3))
    return (xp, xs.reshape(Bs, 1, D),
            jnp.stack(pool_p, axis=1), jnp.stack(pool_s, axis=1),
            jnp.stack(conv_p, axis=1), jnp.stack(conv_s, axis=1),
            seq_minor(kv_prompt[0]), seq_minor(kv_prompt[1]),
            jnp.stack(ks_l, axis=1), jnp.stack(vs_l, axis=1))
```

```python
import functools
import math

import jax
import jax.numpy as jnp
from jax import lax
from jax.experimental import pallas as pl
from jax.experimental.pallas import tpu as pltpu

D_MODEL = 1024
N_HEADS = 16
HEAD_DIM = D_MODEL // N_HEADS
POOL_WINDOWS = (2, 4, 8, 16)
POOL_GROUP = D_MODEL // len(POOL_WINDOWS)
POOL_BUF = max(POOL_WINDOWS) - 1
CONV_W = 3
EPS = 1e-6

V7X_LANES = 128
V7X_SUBLANES = 8
V7X_VMEM_BYTES = 64 * 1024 * 1024

ROW_TILE = 512
FF_TILE = 1408
POOL_TILE = 256
POOL_SUBTILES = 2
POOL_HALO = 128
ATT_BLOCK = 256
ATT_Q_ROWS = 2 * ATT_BLOCK
HEADS_PER_STEP = V7X_LANES // HEAD_DIM
PAGES_PER_STEP = 8

_F32 = jnp.float32
_BF16 = jnp.bfloat16


def _cparams(semantics, vmem_bytes):
    assert vmem_bytes <= V7X_VMEM_BYTES
    return pltpu.CompilerParams(dimension_semantics=semantics, vmem_limit_bytes=int(vmem_bytes))


def _rms(x, g):
    return x * lax.rsqrt(jnp.mean(x * x, axis=-1, keepdims=True) + EPS) * g


def _split_bf16(x):
    hi = x.astype(_BF16)
    lo = (x - hi.astype(_F32)).astype(_BF16)
    return hi, lo


def _dot(a, b):
    return jnp.dot(a, b, preferred_element_type=_F32)


def _softplus(z):
    return jnp.maximum(z, 0.0) + jnp.log(1.0 + jnp.exp(-jnp.abs(z)))


def _pool_prompt_kernel(x_ref, halo_ref, band_ref, wgrp_ref, scale_ref, npre_ref, npost_ref,
                        out_ref, hlast_ref, mix_scr):
    i = pl.program_id(1)
    halo = _rms(halo_ref[0], npre_ref[...])
    halo = jnp.where(i == 0, 0.0, halo)
    for s in range(POOL_SUBTILES):
        rows = slice(s * POOL_TILE, (s + 1) * POOL_TILE)
        xt = x_ref[0, rows, :]
        h = _rms(xt, npre_ref[...])
        hext = jnp.concatenate([halo, h], axis=0)
        hi, lo = _split_bf16(hext)
        pos = (i * POOL_SUBTILES + s) * POOL_TILE + lax.broadcasted_iota(jnp.int32, (POOL_TILE, 1), 0)
        for g, w in enumerate(POOL_WINDOWS):
            cols = slice(g * POOL_GROUP, (g + 1) * POOL_GROUP)
            band = band_ref[g]
            wsum = _dot(band, hi[:, cols]) + _dot(band, lo[:, cols])
            cnt = jnp.minimum(pos + 1, w).astype(_F32)
            diff = wsum / cnt - h[:, cols]
            mix_scr[rows, cols] = _dot(diff.astype(_BF16), wgrp_ref[g])
        mixed = mix_scr[rows, :] * scale_ref[...]
        out_ref[0, rows, :] = xt + _rms(mixed, npost_ref[...])
        halo = h[POOL_TILE - POOL_HALO:, :]
    hlast_ref[0] = h[POOL_TILE - 2 * V7X_SUBLANES:, :]


def _pool_prompt(x, band, wgrp, scale, npre, npost):
    B, S, D = x.shape
    step = POOL_SUBTILES * POOL_TILE
    nt = S // step
    halo_per_step = step // POOL_HALO
    keep = 2 * V7X_SUBLANES
    vec = pl.BlockSpec((1, D), lambda b, i: (0, 0))
    return pl.pallas_call(
        _pool_prompt_kernel,
        grid=(B, nt),
        in_specs=[
            pl.BlockSpec((1, step, D), lambda b, i: (b, i, 0)),
            pl.BlockSpec((1, POOL_HALO, D), lambda b, i: (b, jnp.maximum(i * halo_per_step - 1, 0), 0)),
            pl.BlockSpec(band.shape, lambda b, i: (0, 0, 0)),
            pl.BlockSpec(wgrp.shape, lambda b, i: (0, 0, 0)),
            vec, vec, vec,
        ],
        out_specs=[
            pl.BlockSpec((1, step, D), lambda b, i: (b, i, 0)),
            pl.BlockSpec((1, keep, D), lambda b, i: (b, i, 0)),
        ],
        out_shape=[jax.ShapeDtypeStruct((B, S, D), _F32),
                   jax.ShapeDtypeStruct((B, nt * keep, D), _F32)],
        scratch_shapes=[pltpu.VMEM((step, D), _F32)],
        compiler_params=_cparams(("arbitrary", "arbitrary"), 32 * 2**20),
        name="pool_prompt",
    )(x, x, band, wgrp, scale, npre, npost)


def _pool_sample_kernel(x_ref, buf_ref, wgrp_ref, scale_ref, npre_ref, npost_ref,
                        out_ref, h_ref, mix_scr, *, past):
    xt = x_ref[...]
    h = _rms(xt, npre_ref[...])
    row = lax.broadcasted_iota(jnp.int32, (1, POOL_BUF, 1), 1)
    for g, w in enumerate(POOL_WINDOWS):
        cols = slice(g * POOL_GROUP, (g + 1) * POOL_GROUP)
        in_window = (row >= POOL_BUF - (w - 1)).astype(_F32)
        wsum = jnp.sum(buf_ref[:, :, cols] * in_window, axis=1) + h[:, cols]
        diff = wsum / float(min(past + 1, w)) - h[:, cols]
        mix_scr[:, cols] = _dot(diff.astype(_BF16), wgrp_ref[g])
    mixed = mix_scr[...] * scale_ref[...]
    out_ref[...] = xt + _rms(mixed, npost_ref[...])
    h_ref[...] = h


def _pool_sample(x, buf, wgrp, scale, npre, npost, past):
    M, D = x.shape
    tb = 32
    vec = pl.BlockSpec((1, D), lambda i: (0, 0))
    return pl.pallas_call(
        functools.partial(_pool_sample_kernel, past=past),
        grid=(M // tb,),
        in_specs=[
            pl.BlockSpec((tb, D), lambda i: (i, 0)),
            pl.BlockSpec((tb, POOL_BUF, D), lambda i: (i, 0, 0)),
            pl.BlockSpec(wgrp.shape, lambda i: (0, 0, 0)),
            vec, vec, vec,
        ],
        out_specs=[pl.BlockSpec((tb, D), lambda i: (i, 0)), pl.BlockSpec((tb, D), lambda i: (i, 0))],
        out_shape=[jax.ShapeDtypeStruct((M, D), _F32), jax.ShapeDtypeStruct((M, D), _F32)],
        scratch_shapes=[pltpu.VMEM((tb, D), _F32)],
        compiler_params=_cparams(("arbitrary",), 32 * 2**20),
        name="pool_sample",
    )(x, buf, wgrp, scale, npre, npost)


def _qkv_kernel(*refs, seq_major_kv, first_layer_of):
    x_ref, npre_ref, w_ref = refs[:3]
    q_ref, k_ref, v_ref = refs[-5:-2] if seq_major_kv else refs[-3:]
    hb = _rms(x_ref[...], npre_ref[...]).astype(_BF16)
    D = D_MODEL
    q = _dot(hb, w_ref[:, 0:D])
    q_ref[...] = (q * (HEAD_DIM ** -0.5)).astype(_BF16)
    for j, out_ref in enumerate((k_ref, v_ref)):
        kv = _dot(hb, w_ref[:, (j + 1) * D:(j + 2) * D])
        if not seq_major_kv:
            out_ref[...] = kv
            continue
        refs[-2 + j][...] = kv.astype(_BF16)
        kv_t = kv.T.reshape(N_HEADS, HEAD_DIM, kv.shape[0])
        if first_layer_of is None:
            out_ref[...] = kv_t
        else:
            layer, n_layers = first_layer_of
            for l in range(n_layers):
                out_ref[l] = kv_t if l == layer else jnp.zeros_like(kv_t)


def _qkv(x, npre, w, tm, leaves=None):
    M, D = x.shape
    row = lambda dt: (pl.BlockSpec((tm, D), lambda i: (i, 0)), jax.ShapeDtypeStruct((M, D), dt))
    in_specs = [pl.BlockSpec((tm, D), lambda i: (i, 0)),
                pl.BlockSpec((1, D), lambda i: (0, 0)),
                pl.BlockSpec(w.shape, lambda i: (0, 0))]
    args = [x, npre, w]
    aliases = {}
    first_layer_of = None
    if leaves is None:
        outs = [row(_BF16), row(_F32), row(_F32)]
    else:
        seq_len, n_layers, layer, earlier = leaves
        tps = seq_len // tm
        leaf_shape = jax.ShapeDtypeStruct((M // seq_len, n_layers, N_HEADS, HEAD_DIM, seq_len), _F32)
        if earlier is None:
            first_layer_of = (layer, n_layers)
            leaf_spec = pl.BlockSpec((None, n_layers, N_HEADS, HEAD_DIM, tm), lambda i: (i // tps, 0, 0, 0, i % tps))
        else:
            leaf_spec = pl.BlockSpec((None, None, N_HEADS, HEAD_DIM, tm), lambda i: (i // tps, layer, 0, 0, i % tps))
            in_specs += [pl.BlockSpec(memory_space=pl.ANY)] * 2
            args += list(earlier)
            aliases = {3: 1, 4: 2}
        outs = [row(_BF16), (leaf_spec, leaf_shape), (leaf_spec, leaf_shape), row(_BF16), row(_BF16)]
    return pl.pallas_call(
        functools.partial(_qkv_kernel, seq_major_kv=leaves is not None, first_layer_of=first_layer_of),
        grid=(M // tm,),
        in_specs=in_specs,
        out_specs=[o[0] for o in outs],
        out_shape=[o[1] for o in outs],
        input_output_aliases=aliases,
        compiler_params=_cparams(("arbitrary",), 56 * 2**20),
        name="rms_qkv",
    )(*args)


def _attn_block(qh, k_blk, v_blk, ntri_ref, bias, carry, key_minus_query):
    z = lax.dot_general(qh, k_blk, (((1,), (1,)), ((), ())), preferred_element_type=_F32) + bias
    sp = _softplus(z)
    log_beta = z - sp
    if key_minus_query is not None:
        shape = z.shape
        valid = (lax.broadcasted_iota(jnp.int32, shape, 1) + key_minus_query
                 < lax.broadcasted_iota(jnp.int32, shape, 0))
        sp = jnp.where(valid, sp, 0.0)
    a = jnp.exp(log_beta + _dot(sp.astype(_BF16), ntri_ref[...]) + carry)
    if key_minus_query is not None:
        a = jnp.where(valid, a, 0.0)
    return _dot(a.astype(_BF16), v_blk), jnp.sum(sp, axis=-1, keepdims=True)


def _attn_prompt_kernel(bias_ref, q_ref, k_ref, v_ref, ntri_ref, o_ref, qm_scr, acc_scr, carry_scr):
    hp = pl.program_id(1)
    qi = pl.program_id(2)
    n = ATT_BLOCK
    heads = range(HEADS_PER_STEP)
    q = q_ref[0]
    lane = lax.broadcasted_iota(jnp.int32, (1, V7X_LANES), 1)
    in_head = [(lane >= hh * HEAD_DIM) & (lane < (hh + 1) * HEAD_DIM) for hh in heads]
    for hh in heads:
        qm_scr[hh] = jnp.where(in_head[hh], q, jnp.zeros_like(q))
    bias = [bias_ref[hp * HEADS_PER_STEP + hh] for hh in heads]

    def two_blocks(kv0, kv1, offsets):
        for hh in heads:
            c0 = 0.0 if offsets else carry_scr[hh]
            pv0, rs0 = _attn_block(qm_scr[hh], *kv0, ntri_ref, bias[hh], c0, offsets[0] if offsets else None)
            c1 = c0 - rs0
            pv1, rs1 = _attn_block(qm_scr[hh], *kv1, ntri_ref, bias[hh], c1, offsets[1] if offsets else None)
            if offsets:
                acc_scr[hh] = pv0 + pv1
            else:
                acc_scr[hh] += pv0 + pv1
            carry_scr[hh] = c1 - rs1

    def kv_static(j):
        return k_ref[0, j * n:(j + 1) * n, :], v_ref[0, j * n:(j + 1) * n, :]

    for steps_before in range(k_ref.shape[1] // ATT_Q_ROWS):
        @pl.when(qi == steps_before)
        def _(steps_before=steps_before):
            first = steps_before * (ATT_Q_ROWS // n)
            two_blocks(kv_static(first + 1), kv_static(first), (n, 0))
            for j0 in range(first - 1, 0, -2):
                two_blocks(kv_static(j0), kv_static(j0 - 1), None)

    out = acc_scr[0]
    for hh in heads[1:]:
        out = jnp.where(in_head[hh], acc_scr[hh], out)
    o_ref[0] = out.astype(o_ref.dtype)


def _attn_prompt(q, kb, vb, ntri, bias):
    B, S, D = q.shape
    n = ATT_Q_ROWS
    H = HEADS_PER_STEP
    return pl.pallas_call(
        _attn_prompt_kernel,
        grid_spec=pltpu.PrefetchScalarGridSpec(
            num_scalar_prefetch=1,
            grid=(B, D // V7X_LANES, S // n),
            in_specs=[
                pl.BlockSpec((1, n, V7X_LANES), lambda b, hp, qi, bias: (b, qi, hp)),
                pl.BlockSpec((1, S, V7X_LANES), lambda b, hp, qi, bias: (b, 0, hp)),
                pl.BlockSpec((1, S, V7X_LANES), lambda b, hp, qi, bias: (b, 0, hp)),
                pl.BlockSpec(ntri.shape, lambda b, hp, qi, bias: (0, 0)),
            ],
            out_specs=pl.BlockSpec((1, n, V7X_LANES), lambda b, hp, qi, bias: (b, qi, hp)),
            scratch_shapes=[pltpu.VMEM((H, n, V7X_LANES), _BF16), pltpu.VMEM((H, n, V7X_LANES), _F32),
                            pltpu.VMEM((H, n, 1), _F32)],
        ),
        out_shape=jax.ShapeDtypeStruct((B, S, D), _BF16),
        compiler_params=_cparams(("arbitrary", "arbitrary", "arbitrary"), 32 * 2**20),
        name="sb_attn_prompt",
    )(bias, q, kb, vb, ntri)


def _attn_sample_kernel(pt_ref, bias_ref, q_ref, ntri_ref, *refs):
    del pt_ref
    P = PAGES_PER_STEP
    k_refs, v_refs = refs[:P], refs[P:2 * P]
    o_ref, acc_scr, carry_scr = refs[2 * P:]
    step = pl.program_id(1)
    page = V7X_LANES

    @pl.when(step == 0)
    def _():
        acc_scr[...] = jnp.zeros_like(acc_scr)
        carry_scr[...] = jnp.zeros_like(carry_scr)

    def pages(page_refs):
        return jnp.concatenate([r[...].reshape(D_MODEL, page).astype(_BF16) for r in page_refs], axis=1)

    z = _dot(q_ref[0], pages(k_refs)) + bias_ref[...]
    sp = _softplus(z)
    hi, lo = _split_bf16(sp)
    carry = carry_scr[...]
    between = []
    for s in range(P):
        cols = slice(s * page, (s + 1) * page)
        between.append(_dot(jnp.concatenate([hi[:, cols], lo[:, cols]], axis=1), ntri_ref[...]) + carry)
        carry = carry - jnp.sum(sp[:, cols], axis=-1, keepdims=True)
    carry_scr[...] = carry
    a = jnp.exp(z - sp + jnp.concatenate(between, axis=1)).astype(_BF16)
    acc_scr[...] += lax.dot_general(a, pages(v_refs), (((1,), (1,)), ((), ())), preferred_element_type=_F32)

    @pl.when(step == pl.num_programs(1) - 1)
    def _():
        head = lax.broadcasted_iota(jnp.int32, (N_HEADS, D_MODEL), 0)
        col_head = lax.broadcasted_iota(jnp.int32, (N_HEADS, D_MODEL), 1) // HEAD_DIM
        o_ref[0] = jnp.sum(jnp.where(head == col_head, acc_scr[...], 0.0), axis=0, keepdims=True)


def _attn_sample(q_bd, cache_kt, cache_vt, page_table, ntri, bias_col, layer):
    M, rows, D = q_bd.shape
    n_pages = page_table.shape[1]
    page = cache_kt.shape[-1]
    P = PAGES_PER_STEP
    assert n_pages % P == 0 and page == V7X_LANES and rows == N_HEADS and D == D_MODEL

    def page_spec(s):
        def index(b, i, pt):
            return (pt[b, n_pages - 1 - (i * P + s)], layer, 0, 0, 0)
        return pl.BlockSpec((None, None, N_HEADS, HEAD_DIM, page), index)

    specs = [page_spec(s) for s in range(P)]
    return pl.pallas_call(
        _attn_sample_kernel,
        grid_spec=pltpu.PrefetchScalarGridSpec(
            num_scalar_prefetch=1,
            grid=(M, n_pages // P),
            in_specs=[
                pl.BlockSpec((rows, 1), lambda b, i, pt: (0, 0)),
                pl.BlockSpec((1, rows, D), lambda b, i, pt: (b, 0, 0)),
                pl.BlockSpec(ntri.shape, lambda b, i, pt: (0, 0)),
            ] + specs + specs,
            out_specs=pl.BlockSpec((1, 1, D), lambda b, i, pt: (b, 0, 0)),
            scratch_shapes=[pltpu.VMEM((rows, D), _F32), pltpu.VMEM((rows, 1), _F32)],
        ),
        out_shape=jax.ShapeDtypeStruct((M, 1, D), _F32),
        compiler_params=_cparams(("arbitrary", "arbitrary"), 56 * 2**20),
        name="sb_attn_sample",
    )(page_table, bias_col, q_bd, ntri, *([cache_kt] * P), *([cache_vt] * P))


def _proj_res_kernel(x_ref, o_ref, w_ref, npost_ref, out_ref):
    m = _dot(o_ref[...], w_ref[...])
    out_ref[...] = x_ref[...] + _rms(m, npost_ref[...])


def _proj_res(x, o, w, npost, tm):
    M, D = x.shape
    return pl.pallas_call(
        _proj_res_kernel,
        grid=(M // tm,),
        in_specs=[pl.BlockSpec((tm, D), lambda i: (i, 0)),
                  pl.BlockSpec((tm, D), lambda i: (i, 0)),
                  pl.BlockSpec(w.shape, lambda i: (0, 0)),
                  pl.BlockSpec((1, D), lambda i: (0, 0))],
        out_specs=pl.BlockSpec((tm, D), lambda i: (i, 0)),
        out_shape=jax.ShapeDtypeStruct((M, D), _F32),
        compiler_params=_cparams(("arbitrary",), 32 * 2**20),
        name="proj_res",
    )(x, o, w, npost)


def _gelu_tanh(c):
    return 0.5 * c * (1.0 + jnp.tanh(math.sqrt(2.0 / math.pi) * (c + 0.044715 * (c * c * c))))


def _ffn_kernel(*refs, tm, tiles_per_seq, n_ff_tiles):
    seq_mode = tiles_per_seq > 0
    nf = n_ff_tiles
    (x_ref, p_ref, npre_ref, npost_ref, nple_ref, wg_ref, wu_ref, cw_ref, cb_ref, wd_ref,
     wpg_ref, wpe_ref) = refs[:12]
    if seq_mode:
        out_ref, gout_ref, hb_scr, acc_scr, carry_scr = refs[12:]
    else:
        s0_ref, s1_ref, out_ref, gout_ref, hb_scr, acc_scr = refs[12:]
    i = pl.program_id(0)
    f = pl.program_id(1)
    sub = V7X_SUBLANES

    @pl.when(f == 0)
    def _():
        hb_scr[...] = _rms(x_ref[...], npre_ref[...]).astype(_BF16)
        acc_scr[...] = jnp.zeros_like(acc_scr)

    hb = hb_scr[...]
    g = _dot(hb, wg_ref[...])
    u = _dot(hb, wu_ref[...])
    if seq_mode:
        tail = g[tm - sub:, :]
        prev = jnp.where(i % tiles_per_seq == 0, 0.0, carry_scr[f])
        carry_scr[f] = tail
        gout_ref[...] = tail
        row = lax.broadcasted_iota(jnp.int32, (sub, FF_TILE), 0)
        g1 = pltpu.roll(g, 1, 0)
        g2 = pltpu.roll(g, 2, 0)
        g1 = jnp.concatenate([jnp.where(row < 1, pltpu.roll(prev, 1, 0), g1[:sub]), g1[sub:]], axis=0)
        g2 = jnp.concatenate([jnp.where(row < 2, pltpu.roll(prev, 2, 0), g2[:sub]), g2[sub:]], axis=0)
    else:
        g2 = s0_ref[...]
        g1 = s1_ref[...]
        gout_ref[...] = g
    c = cb_ref[...] + cw_ref[0:1, :] * g2 + cw_ref[1:2, :] * g1 + cw_ref[2:3, :] * g
    act = (_gelu_tanh(c) * u).astype(_BF16)
    acc_scr[...] += _dot(act, wd_ref[...])

    @pl.when(f == nf - 1)
    def _():
        e = _dot(p_ref[...].astype(_BF16), wpe_ref[...])
        x1 = x_ref[...] + _rms(acc_scr[...], npost_ref[...])
        gate = 1.0 / (1.0 + jnp.exp(-_dot(x1.astype(_BF16), wpg_ref[...])))
        out_ref[...] = x1 + _rms(gate * e, nple_ref[...])


def _ffn(x, p, layer, npre, npost, nple, wg, wu, cw, cb, wd, wpg, wpe, *, tm, tiles_per_seq, prev_rows=None):
    M, D = x.shape
    F = wg.shape[2]
    nf = F // FF_TILE
    nt = M // tm
    seq_mode = tiles_per_seq > 0
    vec = pl.BlockSpec((1, D), lambda i, f: (0, 0))
    in_specs = [
        pl.BlockSpec((tm, D), lambda i, f: (i, 0)),
        pl.BlockSpec((None, tm, p.shape[2]), lambda i, f: (layer, i, 0)),
        vec, vec, vec,
        pl.BlockSpec((None, D, FF_TILE), lambda i, f: (layer, 0, f)),
        pl.BlockSpec((None, D, FF_TILE), lambda i, f: (layer, 0, f)),
        pl.BlockSpec((None, CONV_W, FF_TILE), lambda i, f: (layer, 0, f)),
        pl.BlockSpec((None, 1, FF_TILE), lambda i, f: (layer, 0, f)),
        pl.BlockSpec((None, FF_TILE, D), lambda i, f: (layer, f, 0)),
        pl.BlockSpec((None,) + wpg.shape[1:], lambda i, f: (layer, 0, 0)),
        pl.BlockSpec((None,) + wpe.shape[1:], lambda i, f: (layer, 0, 0)),
    ]
    args = [x, p, npre, npost, nple, wg, wu, cw, cb, wd, wpg, wpe]
    scratch = [pltpu.VMEM((tm, D), _BF16), pltpu.VMEM((tm, D), _F32)]
    if seq_mode:
        g_rows = V7X_SUBLANES
        scratch.append(pltpu.VMEM((nf, V7X_SUBLANES, FF_TILE), _F32))
    else:
        g_rows = tm
        in_specs += [pl.BlockSpec((tm, FF_TILE), lambda i, f: (i, f))] * 2
        args += list(prev_rows)
    return pl.pallas_call(
        functools.partial(_ffn_kernel, tm=tm, tiles_per_seq=tiles_per_seq, n_ff_tiles=nf),
        grid=(nt, nf),
        in_specs=in_specs,
        out_specs=[pl.BlockSpec((tm, D), lambda i, f: (i, 0)),
                   pl.BlockSpec((g_rows, FF_TILE), lambda i, f: (i, f))],
        out_shape=[jax.ShapeDtypeStruct((M, D), _F32),
                   jax.ShapeDtypeStruct((nt * g_rows, F), _F32)],
        scratch_shapes=scratch,
        compiler_params=_cparams(("arbitrary", "arbitrary"), 56 * 2**20),
        name="conv_ffn_ple",
    )(*args)


def _pool_band():
    t = jnp.arange(POOL_TILE)[:, None] + POOL_HALO
    c = jnp.arange(POOL_HALO + POOL_TILE)[None, :]
    return jnp.stack([((c <= t) & (c > t - w)).astype(_BF16) for w in POOL_WINDOWS])


def _later_keys_neg(n, copies):
    later = jnp.arange(n)[:, None] > jnp.arange(n)[None, :]
    return jnp.tile(-later.astype(_BF16), (copies, 1))


def kernel(x_prompt, x_sample, state_pool, state_conv, cache_k, cache_v, page_table, p_prompt, p_sample, norm_mix_pre, norm_mix_post, pool_w, pool_scale, sb_wqkv, sb_bias, sb_wo, norm_ffn_pre, norm_ffn_post, ffn_w_gate, ffn_w_up, ffn_conv_w, ffn_conv_b, ffn_w_down, ple_gate, ple_proj, ple_norm):
    Bp, S, D = x_prompt.shape
    Bs = x_sample.shape[0]
    depth = norm_mix_pre.shape[0]
    d_ff = ffn_w_gate.shape[2]
    past = page_table.shape[1] * cache_k.shape[2]
    Mp = Bp * S
    assert x_sample.shape[1] == 1 and S % ROW_TILE == 0 and S % (POOL_SUBTILES * POOL_TILE) == 0 and S % ATT_Q_ROWS == 0
    assert d_ff % FF_TILE == 0 and D == D_MODEL

    xp = x_prompt
    xs = x_sample.reshape(Bs, D)
    vec = lambda a: a.reshape(1, -1)
    band = _pool_band()
    tri_p = _later_keys_neg(ATT_BLOCK, 1)
    tri_s = _later_keys_neg(cache_k.shape[2], 2)
    head_of_col = (jnp.arange(D) // HEAD_DIM)[None, :]
    cache_kt = jnp.transpose(cache_k, (0, 1, 3, 4, 2))
    cache_vt = jnp.transpose(cache_v, (0, 1, 3, 4, 2))

    ffn_stacked = (ffn_w_gate.astype(_BF16), ffn_w_up.astype(_BF16), ffn_conv_w,
                   ffn_conv_b.reshape(depth, 1, d_ff), ffn_w_down.astype(_BF16),
                   ple_gate.astype(_BF16), ple_proj.astype(_BF16))

    pool_p, pool_s, conv_p, conv_s = [], [], [], []
    ks_l, vs_l = [], []
    kv_prompt = None
    for i in range(depth):
        j = i // 2
        npre, npost = vec(norm_mix_pre[i]), vec(norm_mix_post[i])
        if i % 2 == 0:
            wgrp = pool_w[j].astype(_BF16)
            scale = vec(pool_scale[j])
            xp, hlast = _pool_prompt(xp, band, wgrp, scale, npre, npost)
            pool_p.append(hlast[:, -POOL_BUF:, :])
            buf = state_pool[:, j]
            xs, hs = _pool_sample(xs, buf, wgrp, scale, npre, npost, past)
            pool_s.append(jnp.concatenate([buf[:, 1:], hs[:, None, :]], axis=1))
        else:
            wqkv = sb_wqkv[j].astype(_BF16)
            wo = sb_wo[j].astype(_BF16)
            q, kt, vt, kb, vb = _qkv(xp.reshape(Mp, D), npre, wqkv, ROW_TILE, leaves=(S, depth // 2, j, kv_prompt))
            kv_prompt = (kt, vt)
            o = _attn_prompt(q.reshape(Bp, S, D), kb.reshape(Bp, S, D), vb.reshape(Bp, S, D), tri_p, sb_bias[j])
            xp = _proj_res(xp.reshape(Mp, D), o.reshape(Mp, D), wo, npost, ROW_TILE).reshape(Bp, S, D)

            qs, ksn, vsn = _qkv(xs, npre, wqkv, Bs)
            q_bd = jnp.where(head_of_col[None] == jnp.arange(N_HEADS)[None, :, None], qs[:, None, :], 0)
            os_ = _attn_sample(q_bd, cache_kt, cache_vt, page_table, tri_s, sb_bias[j].reshape(N_HEADS, 1), j)
            xs = _proj_res(xs, os_.reshape(Bs, D).astype(_BF16), wo, npost, Bs)
            ks_l.append(ksn.reshape(Bs, 1, N_HEADS, HEAD_DIM))
            vs_l.append(vsn.reshape(Bs, 1, N_HEADS, HEAD_DIM))

        ffn_w = (i, vec(norm_ffn_pre[i]), vec(norm_ffn_post[i]), vec(ple_norm[i])) + ffn_stacked
        xp2, gtail = _ffn(xp.reshape(Mp, D), p_prompt.reshape(depth, Mp, -1), *ffn_w,
                          tm=ROW_TILE, tiles_per_seq=S // ROW_TILE)
        xp = xp2.reshape(Bp, S, D)
        gtail = gtail.reshape(Bp, S // ROW_TILE, V7X_SUBLANES, d_ff)
        conv_p.append(gtail[:, -1, -(CONV_W - 1):, :])
        sc = state_conv[:, i]
        xs, gs = _ffn(xs, p_sample.reshape(depth, Bs, -1), *ffn_w, tm=Bs, tiles_per_seq=0,
                      prev_rows=(sc[:, 0], sc[:, 1]))
        conv_s.append(jnp.stack([sc[:, 1], gs], axis=1))

    seq_minor = lambda a: jnp.transpose(a, (0, 1, 4, 2, 3))
    return (xp, xs.reshape(Bs, 1, D),
            jnp.stack(pool_p, axis=1), jnp.stack(pool_s, axis=1),
            jnp.stack(conv_p, axis=1), jnp.stack(conv_s, axis=1),
            seq_minor(kv_prompt[0]), seq_minor(kv_prompt[1]),
            jnp.stack(ks_l, axis=1), jnp.stack(vs_l, axis=1))
```
